```python
import jax, jax.numpy as jnp
from jax import lax
import numpy as np

D_MODEL = 4096
BATCH = 2
SEQ = 8192
DEPTH = 4

N_META = 16
GRID_W = 64
EPS = 1e-6
D_FF = 5632
ATTN_HEADS = 16
ATTN_KV_HEADS = 4
HEAD_DIM = 128
Q_BLOCK = 128
ROPE_THETA = 10000.0
FOURIER_GROUPS = 16
FOURIER_GROUP_DIM = 128
GLA_HEADS = 8
GLA_DK = D_MODEL // (2 * GLA_HEADS)
GLA_DV = D_MODEL // GLA_HEADS
GLA_GATE_RANK = 16
GLA_GATE_TAU = 16.0
GLA_CHUNK = 64

ATTN_Q_W = ATTN_HEADS * HEAD_DIM
ATTN_KV_W = ATTN_KV_HEADS * HEAD_DIM
FOURIER_W = FOURIER_GROUPS * FOURIER_GROUP_DIM
EVEN_IN_W = ATTN_Q_W + 2 * ATTN_KV_W + FOURIER_W
EVEN_MIX_W = ATTN_Q_W + FOURIER_W
GLA_K_W = GLA_HEADS * GLA_DK
GLA_V_W = GLA_HEADS * GLA_DV
ODD_IN_W = 2 * GLA_K_W + 2 * GLA_V_W

kernel_name = 'hybrid_gqa_fourier_gla_macaron_encoder'


def rmsnorm(x, g):
    xf = x.astype(jnp.float32)
    y = xf * lax.rsqrt(jnp.mean(xf * xf, axis=-1, keepdims=True) + EPS)
    return (y * g.astype(jnp.float32)).astype(x.dtype)


def swiglu(x, w_gate, w_up, w_down):
    return (jax.nn.silu(x @ w_gate) * (x @ w_up)) @ w_down


def axial_rope_tables(n_real):
    n_rows = n_real // GRID_W
    rows = jnp.repeat(jnp.arange(n_rows, dtype=jnp.float32), GRID_W)
    cols = jnp.tile(jnp.arange(GRID_W, dtype=jnp.float32), n_rows)
    n_freq = HEAD_DIM // 4
    inv_freq = jnp.power(ROPE_THETA, -jnp.arange(n_freq, dtype=jnp.float32) / n_freq)
    ang = jnp.stack([rows[:, None] * inv_freq, cols[:, None] * inv_freq], axis=1)
    ang = jnp.concatenate([jnp.zeros((N_META, 2, n_freq), jnp.float32), ang], axis=0)[:, None]
    return jnp.cos(ang), jnp.sin(ang)


def apply_axial_rope(x, cos, sin):
    shp = x.shape
    xr = x.astype(jnp.float32).reshape(shp[:-1] + (2, 2, HEAD_DIM // 4))
    x0, x1 = xr[..., 0, :], xr[..., 1, :]
    y = jnp.stack([x0 * cos - x1 * sin, x0 * sin + x1 * cos], axis=-2)
    return y.reshape(shp).astype(x.dtype)


def block_attention(q, k, v):
    B, L = q.shape[0], q.shape[1]
    n_pad = (-L) % Q_BLOCK
    pad = lambda a: jnp.pad(a, ((0, 0), (n_pad, 0), (0, 0), (0, 0)))
    q, k, v = pad(q), pad(k), pad(v)
    Lp = L + n_pad
    nb = Lp // Q_BLOCK
    G = ATTN_HEADS // ATTN_KV_HEADS
    scale = HEAD_DIM ** -0.5
    key_bias = jnp.where(jnp.arange(Lp) < n_pad, -1e30, 0.0).astype(jnp.float32)
    qb = q.reshape(B, nb, Q_BLOCK, ATTN_KV_HEADS, G, HEAD_DIM).transpose(1, 0, 2, 3, 4, 5)

    def one_block(q_blk):
        s = jnp.einsum('bqkgd,bskd->bkgqs', q_blk, k, preferred_element_type=jnp.float32) * scale + key_bias
        p = jax.nn.softmax(s, axis=-1).astype(v.dtype)
        return jnp.einsum('bkgqs,bskd->bqkgd', p, v)

    o = lax.map(one_block, qb)
    o = o.transpose(1, 0, 2, 3, 4, 5).reshape(B, Lp, ATTN_Q_W)
    return o[:, n_pad:]


def fourier_mix(f):
    B, L = f.shape[0], f.shape[1]
    fg = f.astype(jnp.float32).reshape(B, L, FOURIER_GROUPS, FOURIER_GROUP_DIM)
    y = jnp.fft.fft2(fg, axes=(1, 3), norm='ortho').real
    return y.reshape(B, L, FOURIER_W).astype(f.dtype)


def even_mixer(h, w_in, q_norm, k_norm, w_out, cos, sin):
    B, L, _ = h.shape
    u = h @ w_in
    q, k, v, f = jnp.split(u, [ATTN_Q_W, ATTN_Q_W + ATTN_KV_W, ATTN_Q_W + 2 * ATTN_KV_W], axis=-1)
    q = apply_axial_rope(rmsnorm(q.reshape(B, L, ATTN_HEADS, HEAD_DIM), q_norm), cos, sin)
    k = apply_axial_rope(rmsnorm(k.reshape(B, L, ATTN_KV_HEADS, HEAD_DIM), k_norm), cos, sin)
    v = v.reshape(B, L, ATTN_KV_HEADS, HEAD_DIM)
    o = jnp.concatenate([block_attention(q, k, v), fourier_mix(f)], axis=-1)
    return o @ w_out


def gla_chunk_scan(q, k, v, log_a):
    B, Lp, H, DK = q.shape
    DV = v.shape[-1]
    C = GLA_CHUNK
    nc = Lp // C
    to_chunks = lambda a: a.reshape(B, nc, C, H, a.shape[-1]).transpose(1, 0, 3, 2, 4)
    causal = jnp.tril(jnp.ones((C, C), dtype=bool))[:, :, None]

    def step(S, xs):
        qi, ki, vi, gi = xs
        qf, kf, vf = qi.astype(jnp.float32), ki.astype(jnp.float32), vi.astype(jnp.float32)
        b = jnp.cumsum(gi, axis=2)
        inter = jnp.einsum('bhcd,bhde->bhce', qf * jnp.exp(b), S)
        diff = b[:, :, :, None, :] - b[:, :, None, :, :]
        decay = jnp.exp(jnp.where(causal, diff, -jnp.inf))
        A = jnp.einsum('bhid,bhjd,bhijd->bhij', qf, kf, decay)
        intra = jnp.einsum('bhij,bhje->bhie', A, vf)
        b_last = b[:, :, -1:, :]
        S_new = jnp.exp(b_last[:, :, 0, :])[..., None] * S + jnp.einsum('bhcd,bhce->bhde', kf * jnp.exp(b_last - b), vf)
        return S_new, (inter + intra).astype(v.dtype)

    S0 = jnp.zeros((B, H, DK, DV), jnp.float32)
    _, o = lax.scan(step, S0, (to_chunks(q), to_chunks(k), to_chunks(v), to_chunks(log_a)))
    return o.transpose(1, 0, 3, 2, 4).reshape(B, Lp, H, DV)


def odd_mixer(h, w_in, gate_a, gate_b, gate_bias, head_norm, w_out):
    B, L, _ = h.shape
    u = h @ w_in
    q, k, v, r = jnp.split(u, [GLA_K_W, 2 * GLA_K_W, 2 * GLA_K_W + GLA_V_W], axis=-1)
    q = q.reshape(B, L, GLA_HEADS, GLA_DK) * (GLA_DK ** -0.5)
    k = k.reshape(B, L, GLA_HEADS, GLA_DK)
    v = v.reshape(B, L, GLA_HEADS, GLA_DV)
    low = jnp.einsum('bld,ndr->nblr', h, gate_a)
    z = jnp.einsum('nblr,nrk->nblk', low, gate_b) + gate_bias[:, None, None, :]
    log_a = (jax.nn.log_sigmoid(z.astype(jnp.float32)) / GLA_GATE_TAU).reshape(2, B, L, GLA_HEADS, GLA_DK)
    n_pad = (-L) % GLA_CHUNK
    pad = lambda a: jnp.pad(a, ((0, 0), (n_pad, 0), (0, 0), (0, 0)))
    q, k, v, g_fw, g_bw = pad(q), pad(k), pad(v), pad(log_a[0]), pad(log_a[1])
    flip = lambda a: a[:, ::-1]
    o_fw = gla_chunk_scan(q, k, v, g_fw)
    o_bw = flip(gla_chunk_scan(flip(q), flip(k), flip(v), flip(g_bw)))
    o = rmsnorm((o_fw + o_bw)[:, n_pad:], head_norm)
    o = o.reshape(B, L, GLA_V_W) * jax.nn.silu(r)
    return o @ w_out


def setup_inputs(seed: int = 0) -> dict:
    key = jax.random.key(seed)
    ks = jax.random.split(key, 16)
    n_even = (DEPTH + 1) // 2
    n_odd = DEPTH // 2
    nrm = lambda k, shape, fan_in: jax.random.normal(k, shape, jnp.float32) * (fan_in ** -0.5)
    gain = lambda k, shape: 1.0 + 0.02 * jax.random.normal(k, shape, jnp.float32)
    return {
        'x': jax.random.normal(ks[0], (BATCH, SEQ, D_MODEL), jnp.float32),
        'meta_tokens': jax.random.normal(ks[1], (N_META, D_MODEL), jnp.float32),
        'pre_norm': gain(ks[2], (DEPTH, 3, D_MODEL)),
        'ffn_w_gate': nrm(ks[3], (DEPTH, 2, D_MODEL, D_FF), D_MODEL),
        'ffn_w_up': nrm(ks[4], (DEPTH, 2, D_MODEL, D_FF), D_MODEL),
        'ffn_w_down': nrm(ks[5], (DEPTH, 2, D_FF, D_MODEL), D_FF),
        'even_w_in': nrm(ks[6], (n_even, D_MODEL, EVEN_IN_W), D_MODEL),
        'even_q_norm': gain(ks[7], (n_even, HEAD_DIM)),
        'even_k_norm': gain(ks[8], (n_even, HEAD_DIM)),
        'even_w_out': nrm(ks[9], (n_even, EVEN_MIX_W, D_MODEL), EVEN_MIX_W),
        'odd_w_in': nrm(ks[10], (n_odd, D_MODEL, ODD_IN_W), D_MODEL),
        'odd_gate_a': nrm(ks[11], (n_odd, 2, D_MODEL, GLA_GATE_RANK), D_MODEL),
        'odd_gate_b': nrm(ks[12], (n_odd, 2, GLA_GATE_RANK, GLA_K_W), GLA_GATE_RANK),
        'odd_gate_bias': 0.1 * jax.random.normal(ks[13], (n_odd, 2, GLA_K_W), jnp.float32),
        'odd_head_norm': gain(ks[14], (n_odd, GLA_DV)),
        'odd_w_out': nrm(ks[15], (n_odd, GLA_V_W, D_MODEL), GLA_V_W),
    }


def reference(x, meta_tokens, pre_norm, ffn_w_gate, ffn_w_up, ffn_w_down, even_w_in, even_q_norm, even_k_norm, even_w_out, odd_w_in, odd_gate_a, odd_gate_b, odd_gate_bias, odd_head_norm, odd_w_out):
    B, n_real, _ = x.shape
    cos, sin = axial_rope_tables(n_real)
    meta = jnp.broadcast_to(meta_tokens.astype(x.dtype)[None], (B, N_META, D_MODEL))
    h = jnp.concatenate([meta, x], axis=1)
    for layer in range(DEPTH):
        i = layer // 2
        h = h + 0.5 * swiglu(rmsnorm(h, pre_norm[layer, 0]), ffn_w_gate[layer, 0], ffn_w_up[layer, 0], ffn_w_down[layer, 0])
        hn = rmsnorm(h, pre_norm[layer, 1])
        if layer % 2 == 0:
            h = h + even_mixer(hn, even_w_in[i], even_q_norm[i], even_k_norm[i], even_w_out[i], cos, sin)
        else:
            h = h + odd_mixer(hn, odd_w_in[i], odd_gate_a[i], odd_gate_b[i], odd_gate_bias[i], odd_head_norm[i], odd_w_out[i])
        h = h + 0.5 * swiglu(rmsnorm(h, pre_norm[layer, 2]), ffn_w_gate[layer, 1], ffn_w_up[layer, 1], ffn_w_down[layer, 1])
    return h[:, N_META:]
```

```python
import functools
import math
from typing import NamedTuple

import jax
import jax.numpy as jnp
from jax import lax
from jax.experimental import pallas as pl
from jax.experimental.pallas import tpu as pltpu

F32 = jnp.float32
BF16 = jnp.bfloat16

N_META = 16
GRID_W = 64
EPS = 1e-6
ATTN_HEADS = 16
ATTN_KV_HEADS = 4
HEAD_DIM = 128
Q_BLOCK = 128
ROPE_THETA = 10000.0
FOURIER_GROUPS = 16
FOURIER_GROUP_DIM = 128
GLA_HEADS = 8
GLA_GATE_TAU = 16.0

V7X_VMEM_BYTES = 64 * 1024 * 1024
VMEM_LIMIT_BYTES = V7X_VMEM_BYTES - 8 * 1024 * 1024
LANES = 128
NEG_BIG = -1e30


class Cfg(NamedTuple):
    batch: int
    seq: int
    d_model: int
    d_ff: int
    depth: int
    n_meta: int
    grid_w: int
    attn_heads: int
    attn_kv_heads: int
    head_dim: int
    q_block: int
    four_groups: int
    four_dim: int
    gla_heads: int
    gla_rank: int
    tm_norm: int
    tm: int
    tn: int
    attn_tk: int
    dft_tm: int
    dft_tn: int
    dft_tk: int
    gla_chunk: int
    gla_sub: int

    @property
    def length(self):
        return self.n_meta + self.seq

    @property
    def n_pad(self):
        return (-self.length) % self.q_block

    @property
    def lp(self):
        return self.length + self.n_pad

    @property
    def rows(self):
        return self.batch * self.lp

    @property
    def q_w(self):
        return self.attn_heads * self.head_dim

    @property
    def kv_w(self):
        return self.attn_kv_heads * self.head_dim

    @property
    def four_w(self):
        return self.four_groups * self.four_dim

    @property
    def gla_dk(self):
        return self.d_model // (2 * self.gla_heads)

    @property
    def gla_dv(self):
        return self.d_model // self.gla_heads


def _params(*sem):
    return pltpu.CompilerParams(dimension_semantics=sem, vmem_limit_bytes=VMEM_LIMIT_BYTES)


def _dot(a, b):
    return jnp.dot(a, b, preferred_element_type=F32)


def _dot_nt(a, b):
    return lax.dot_general(a, b, (((1,), (1,)), ((), ())), preferred_element_type=F32)


def _dot_tn(a, b):
    return lax.dot_general(a, b, (((0,), (0,)), ((), ())), preferred_element_type=F32)


def _rmsnorm_kernel(x_ref, g_ref, o_ref, *, eps):
    x = x_ref[...]
    ms = jnp.mean(x * x, axis=-1, keepdims=True)
    o_ref[...] = (x * lax.rsqrt(ms + eps) * g_ref[...]).astype(o_ref.dtype)


def _rmsnorm(cfg, h, gain):
    m, d = h.shape
    tm = cfg.tm_norm
    return pl.pallas_call(
        functools.partial(_rmsnorm_kernel, eps=EPS),
        out_shape=jax.ShapeDtypeStruct((m, d), BF16),
        grid=(m // tm,),
        in_specs=[pl.BlockSpec((tm, d), lambda i: (i, 0)),
                  pl.BlockSpec((1, d), lambda i: (0, 0))],
        out_specs=pl.BlockSpec((tm, d), lambda i: (i, 0)),
        compiler_params=_params("parallel"),
        name="rmsnorm",
    )(h, gain.reshape(1, d))


def _ffn_up_kernel(x_ref, wg_ref, wu_ref, o_ref):
    x = x_ref[...]
    g = _dot(x, wg_ref[...])
    u = _dot(x, wu_ref[...])
    o_ref[...] = (g * jax.nn.sigmoid(g) * u).astype(o_ref.dtype)


def _ffn_up(cfg, x, wg, wu, layer, slot):
    m, d = x.shape
    f = wg.shape[-1]
    tm, tn = cfg.tm, cfg.tn
    wspec = pl.BlockSpec((None, None, d, tn), lambda i, j: (layer, slot, 0, j))
    return pl.pallas_call(
        _ffn_up_kernel,
        out_shape=jax.ShapeDtypeStruct((m, f), BF16),
        grid=(m // tm, f // tn),
        in_specs=[pl.BlockSpec((tm, d), lambda i, j: (i, 0)), wspec, wspec],
        out_specs=pl.BlockSpec((tm, tn), lambda i, j: (i, j)),
        compiler_params=_params("parallel", "arbitrary"),
        name="ffn_up",
    )(x, wg, wu)


def _mm_res_kernel(*refs, n_lhs, scale):
    xs, ws = refs[:n_lhs], refs[n_lhs:2 * n_lhs]
    h_ref, o_ref = refs[2 * n_lhs], refs[2 * n_lhs + 1]
    acc = _dot(xs[0][...], ws[0][...])
    for x_ref, w_ref in zip(xs[1:], ws[1:]):
        acc = acc + _dot(x_ref[...], w_ref[...])
    o_ref[...] = h_ref[...] + scale * acc


def _mm_res(cfg, xs, w, w_lead, h, scale):
    m, d = h.shape
    kx = xs[0].shape[1]
    n_lhs = len(xs)
    tm, tn = cfg.tm, cfg.tn
    n_lead = len(w_lead)
    w_specs = [
        pl.BlockSpec((None,) * n_lead + (kx, tn), lambda i, j, p=p: tuple(w_lead) + (p, j))
        for p in range(n_lhs)
    ]
    return pl.pallas_call(
        functools.partial(_mm_res_kernel, n_lhs=n_lhs, scale=scale),
        out_shape=jax.ShapeDtypeStruct((m, d), F32),
        grid=(m // tm, d // tn),
        in_specs=[pl.BlockSpec((tm, kx), lambda i, j: (i, 0))] * n_lhs + w_specs
        + [pl.BlockSpec((tm, tn), lambda i, j: (i, j))],
        out_specs=pl.BlockSpec((tm, tn), lambda i, j: (i, j)),
        input_output_aliases={2 * n_lhs: 0},
        compiler_params=_params("parallel", "arbitrary"),
        name="mm_residual",
    )(*xs, *([w] * n_lhs), h)


def _even_inproj_kernel(x_ref, w_ref, cos_ref, sin_ref, qg_ref, kg_ref, o_ref, *,
                        n_q_tiles, n_k_tiles, hd, eps, q_scale):
    j = pl.program_id(1)
    acc = _dot(x_ref[...], w_ref[...])
    heads_per_tile = acc.shape[1] // hd

    @pl.when(j < n_q_tiles + n_k_tiles)
    def _():
        gain = jnp.where(j < n_q_tiles, qg_ref[...] * q_scale, kg_ref[...])
        c = cos_ref[...]
        s = sin_ref[...]
        lane = lax.broadcasted_iota(jnp.int32, c.shape, 1)
        first_half = (lane % (hd // 2)) < (hd // 4)
        for hh in range(heads_per_tile):
            xh = acc[:, hh * hd:(hh + 1) * hd]
            ms = jnp.mean(xh * xh, axis=-1, keepdims=True)
            y = xh * lax.rsqrt(ms + eps) * gain
            partner = jnp.where(first_half,
                                pltpu.roll(y, hd - hd // 4, axis=1),
                                pltpu.roll(y, hd // 4, axis=1))
            o_ref[:, hh * hd:(hh + 1) * hd] = (y * c + partner * s).astype(o_ref.dtype)

    @pl.when(j >= n_q_tiles + n_k_tiles)
    def _():
        o_ref[...] = acc.astype(o_ref.dtype)


def _even_inproj(cfg, x, w, layer_i, cos_t, sin_t, q_gain, k_gain):
    m, d = x.shape
    n = w.shape[-1]
    tm, tn, hd = cfg.tm, cfg.tn, cfg.head_dim
    assert cfg.q_w % tn == 0 and cfg.kv_w % tn == 0 and tn % hd == 0
    kern = functools.partial(
        _even_inproj_kernel, n_q_tiles=cfg.q_w // tn, n_k_tiles=cfg.kv_w // tn, hd=hd,
        eps=EPS, q_scale=hd ** -0.5)
    return pl.pallas_call(
        kern,
        out_shape=jax.ShapeDtypeStruct((m, n), BF16),
        grid=(m // tm, n // tn),
        in_specs=[pl.BlockSpec((tm, d), lambda i, j: (i, 0)),
                  pl.BlockSpec((None, d, tn), lambda i, j: (layer_i, 0, j)),
                  pl.BlockSpec((tm, hd), lambda i, j: (i, 0)),
                  pl.BlockSpec((tm, hd), lambda i, j: (i, 0)),
                  pl.BlockSpec((1, hd), lambda i, j: (0, 0)),
                  pl.BlockSpec((1, hd), lambda i, j: (0, 0))],
        out_specs=pl.BlockSpec((tm, tn), lambda i, j: (i, j)),
        compiler_params=_params("parallel", "arbitrary"),
        name="even_inproj",
    )(x, w, cos_t, sin_t, q_gain.reshape(1, hd), k_gain.reshape(1, hd))


def _attn_kernel(q_ref, k_ref, v_ref, o_ref, *, groups, hd, tk, n_kt, n_pad):
    q = q_ref[...]
    tq = q.shape[0]
    q4 = jnp.concatenate([q[:, g * hd:(g + 1) * hd] for g in range(groups)], axis=0)

    def tile(start, masked, carry):
        m_prev, l_prev, acc = carry
        k = k_ref[pl.ds(start, tk), :]
        v = v_ref[pl.ds(start, tk), :]
        s = _dot_nt(q4, k)
        if masked:
            col = lax.broadcasted_iota(jnp.int32, s.shape, 1)
            s = jnp.where(col < n_pad, NEG_BIG, s)
        m_new = jnp.maximum(m_prev, jnp.max(s, axis=-1, keepdims=True))
        alpha = jnp.exp(m_prev - m_new)
        p = jnp.exp(s - m_new)
        l_new = alpha * l_prev + jnp.sum(p, axis=-1, keepdims=True)
        acc = alpha * acc + _dot(p.astype(v.dtype), v)
        return m_new, l_new, acc

    rows = groups * tq
    carry = (jnp.full((rows, 1), NEG_BIG, F32), jnp.zeros((rows, 1), F32),
             jnp.zeros((rows, hd), F32))
    carry = tile(0, True, carry)
    carry = lax.fori_loop(
        1, n_kt, lambda i, c: tile(pl.multiple_of(i * tk, tk), False, c), carry)
    _, l_fin, acc = carry
    o = acc / l_fin
    o_ref[...] = jnp.concatenate(
        [o[g * tq:(g + 1) * tq] for g in range(groups)], axis=1).astype(o_ref.dtype)


def _attention(cfg, u):
    m = u.shape[0]
    hd, kv, tq, lp = cfg.head_dim, cfg.attn_kv_heads, cfg.q_block, cfg.lp
    groups = cfg.attn_heads // kv
    n_q = lp // tq
    tk = cfg.attn_tk
    assert cfg.n_pad < tk and lp % tk == 0
    kern = functools.partial(_attn_kernel, groups=groups, hd=hd, tk=tk, n_kt=lp // tk,
                             n_pad=cfg.n_pad)
    k_col0 = cfg.q_w // hd
    v_col0 = k_col0 + kv
    return pl.pallas_call(
        kern,
        out_shape=jax.ShapeDtypeStruct((m, cfg.q_w), BF16),
        grid=(cfg.batch, kv, n_q),
        in_specs=[pl.BlockSpec((tq, groups * hd), lambda b, h, i: (b * n_q + i, h)),
                  pl.BlockSpec((lp, hd), lambda b, h, i: (b, k_col0 + h)),
                  pl.BlockSpec((lp, hd), lambda b, h, i: (b, v_col0 + h))],
        out_specs=pl.BlockSpec((tq, groups * hd), lambda b, h, i: (b * n_q + i, h)),
        compiler_params=_params("parallel", "parallel", "arbitrary"),
        name="gqa_attention",
    )(u, u, u)


def _chan_dft_kernel(f_ref, cs_ref, a_ref, b_ref, *, gd):
    cs = cs_ref[...]
    for g in range(f_ref.shape[1] // gd):
        ab = _dot(f_ref[:, g * gd:(g + 1) * gd], cs)
        a_ref[:, g * gd:(g + 1) * gd] = ab[:, :gd].astype(a_ref.dtype)
        b_ref[:, g * gd:(g + 1) * gd] = ab[:, gd:].astype(b_ref.dtype)


def _chan_dft(cfg, u, cs):
    m = u.shape[0]
    gd, fw = cfg.four_dim, cfg.four_w
    f0 = cfg.q_w + 2 * cfg.kv_w
    fb = math.gcd(math.gcd(f0, fw), 1024)
    tm = cfg.tm
    out = jax.ShapeDtypeStruct((m, fw), BF16)
    return pl.pallas_call(
        functools.partial(_chan_dft_kernel, gd=gd),
        out_shape=(out, out),
        grid=(m // tm, fw // fb),
        in_specs=[pl.BlockSpec((tm, fb), lambda i, j: (i, f0 // fb + j)),
                  pl.BlockSpec((gd, 2 * gd), lambda i, j: (0, 0))],
        out_specs=(pl.BlockSpec((tm, fb), lambda i, j: (i, j)),
                   pl.BlockSpec((tm, fb), lambda i, j: (i, j))),
        compiler_params=_params("parallel", "parallel"),
        name="fourier_channel_dft",
    )(u, cs)


def _seq_dft_kernel(wc_ref, ws_ref, a_ref, b_ref, o_ref, acc_ref):
    k = pl.program_id(3)

    @pl.when(k == 0)
    def _():
        acc_ref[...] = jnp.zeros_like(acc_ref)

    acc_ref[...] += _dot(wc_ref[...], a_ref[...]) - _dot(ws_ref[...], b_ref[...])

    @pl.when(k == pl.num_programs(3) - 1)
    def _():
        o_ref[...] = acc_ref[...].astype(o_ref.dtype)


def _seq_dft(cfg, wc, ws, a, b):
    m, fw = a.shape
    lp = cfg.lp
    tm, tn, tk = cfg.dft_tm, cfg.dft_tn, cfg.dft_tk
    n_m, n_k = lp // tm, lp // tk
    return pl.pallas_call(
        _seq_dft_kernel,
        out_shape=jax.ShapeDtypeStruct((m, fw), BF16),
        grid=(cfg.batch, fw // tn, n_m, n_k),
        in_specs=[pl.BlockSpec((tm, tk), lambda bb, n, i, k: (i, k)),
                  pl.BlockSpec((tm, tk), lambda bb, n, i, k: (i, k)),
                  pl.BlockSpec((tk, tn), lambda bb, n, i, k: (bb * n_k + k, n)),
                  pl.BlockSpec((tk, tn), lambda bb, n, i, k: (bb * n_k + k, n))],
        out_specs=pl.BlockSpec((tm, tn), lambda bb, n, i, k: (bb * n_m + i, n)),
        scratch_shapes=[pltpu.VMEM((tm, tn), F32)],
        compiler_params=_params("parallel", "parallel", "parallel", "arbitrary"),
        name="fourier_sequence_dft",
    )(wc, ws, a, b)


def _odd_inproj_kernel(x_ref, w_ref, o_ref, *, n_q_tiles, q_scale):
    acc = _dot(x_ref[...], w_ref[...])
    sc = jnp.where(pl.program_id(1) < n_q_tiles, q_scale, 1.0).astype(F32)
    o_ref[...] = (acc * sc).astype(o_ref.dtype)


def _odd_inproj(cfg, x, w, layer_i):
    m, d = x.shape
    n = w.shape[-1]
    tm, tn = cfg.tm, cfg.tn
    kern = functools.partial(_odd_inproj_kernel, n_q_tiles=cfg.gla_heads * cfg.gla_dk // tn,
                             q_scale=cfg.gla_dk ** -0.5)
    return pl.pallas_call(
        kern,
        out_shape=jax.ShapeDtypeStruct((m, n), BF16),
        grid=(m // tm, n // tn),
        in_specs=[pl.BlockSpec((tm, d), lambda i, j: (i, 0)),
                  pl.BlockSpec((None, d, tn), lambda i, j: (layer_i, 0, j))],
        out_specs=pl.BlockSpec((tm, tn), lambda i, j: (i, j)),
        compiler_params=_params("parallel", "arbitrary"),
        name="odd_inproj",
    )(x, w)


def _gate_low_kernel(x_ref, w_ref, o_ref):
    o_ref[...] = _dot(x_ref[...], w_ref[...])


def _gate_low(cfg, x, ga):
    m, d = x.shape
    r2 = ga.shape[1]
    tm = cfg.tm
    return pl.pallas_call(
        _gate_low_kernel,
        out_shape=jax.ShapeDtypeStruct((m, r2), F32),
        grid=(m // tm,),
        in_specs=[pl.BlockSpec((tm, d), lambda i: (i, 0)),
                  pl.BlockSpec((d, r2), lambda i: (0, 0))],
        out_specs=pl.BlockSpec((tm, r2), lambda i: (i, 0)),
        compiler_params=_params("parallel"),
        name="gla_gate_lowrank",
    )(x, ga)


def _gla_kernel(*refs, rev, chunk, sub, n_chunks, n_pad, tau, eps):
    if rev:
        (q_ref, k_ref, v_ref, low_ref, gb_ref, bias_ref, tri_ref, lmask_ref,
         ofw_ref, r_ref, hn_ref, o_ref, s_ref, cum_ref, qf_ref, kf_ref, a_ref) = refs
    else:
        (q_ref, k_ref, v_ref, low_ref, gb_ref, bias_ref, tri_ref, lmask_ref,
         o_ref, s_ref, cum_ref, qf_ref, kf_ref, a_ref) = refs
    c = pl.program_id(2)
    pos_chunk = (n_chunks - 1 - c) if rev else c

    @pl.when(c == 0)
    def _():
        s_ref[...] = jnp.zeros_like(s_ref)

    row = pos_chunk * chunk + lax.broadcasted_iota(jnp.int32, (chunk, 1), 0)
    valid = row >= n_pad

    z = jnp.dot(low_ref[...], gb_ref[...], preferred_element_type=F32,
                precision=lax.Precision.HIGHEST) + bias_ref[...]
    log_a = (jnp.minimum(z, 0.0) - jnp.log(1.0 + jnp.exp(-jnp.abs(z)))) * (1.0 / tau)
    log_a = jnp.where(valid, log_a, 0.0)
    kf = jnp.where(valid, k_ref[...].astype(F32), 0.0)
    qf = q_ref[...].astype(F32)
    v = v_ref[...]

    g_hi = log_a.astype(BF16)
    g_lo = (log_a - g_hi.astype(F32)).astype(BF16)
    tri = tri_ref[...]
    cum = _dot(tri, g_hi) + _dot(tri, g_lo)
    cum_end = cum[0:1] if rev else cum[chunk - 1:chunk]
    cum_ref[...] = cum
    qf_ref[...] = qf
    kf_ref[...] = kf

    s_old = s_ref[...]
    inter = _dot_nt((qf * jnp.exp(cum)).astype(BF16), s_old.astype(BF16))

    a_acc = jnp.zeros((chunk, chunk), F32)
    s_blk, level = chunk // 2, 0
    while s_blk >= sub:
        q_parts, k_parts = [], []
        zeros = jnp.zeros((s_blk, qf.shape[1]), BF16)
        for p in range(0, chunk, 2 * s_blk):
            lo = slice(p, p + s_blk)
            hi = slice(p + s_blk, p + 2 * s_blk)
            if rev:
                ref_row = cum[p + s_blk:p + s_blk + 1]
                q_parts += [(qf[lo] * jnp.exp(cum[lo] - ref_row)).astype(BF16), zeros]
                k_parts += [zeros, (kf[hi] * jnp.exp(ref_row - cum[hi])).astype(BF16)]
            else:
                ref_row = cum[p + s_blk - 1:p + s_blk]
                q_parts += [zeros, (qf[hi] * jnp.exp(cum[hi] - ref_row)).astype(BF16)]
                k_parts += [(kf[lo] * jnp.exp(ref_row - cum[lo])).astype(BF16), zeros]
        a_lvl = _dot_nt(jnp.concatenate(q_parts, axis=0), jnp.concatenate(k_parts, axis=0))
        a_acc = a_acc + a_lvl * lmask_ref[level]
        s_blk //= 2
        level += 1
    a_ref[...] = a_acc

    def diag_block(bi, carry):
        r0 = pl.multiple_of(bi * sub, sub)
        bb = cum_ref[pl.ds(r0, sub), :]
        qq = qf_ref[pl.ds(r0, sub), :]
        lane = lax.broadcasted_iota(jnp.int32, (sub, chunk), 1)
        rowi = r0 + lax.broadcasted_iota(jnp.int32, (sub, chunk), 0)
        ablk = jnp.zeros((sub, chunk), F32)
        for jj in range(sub):
            brow = cum_ref[pl.ds(r0 + jj, 1), :]
            krow = kf_ref[pl.ds(r0 + jj, 1), :]
            e = jnp.exp(jnp.minimum(bb - brow, 0.0))
            col = jnp.sum(qq * e * krow, axis=-1, keepdims=True)
            ablk = jnp.where(lane == r0 + jj, col, ablk)
        keep = (lane >= rowi) if rev else (lane <= rowi)
        a_ref[pl.ds(r0, sub), :] += jnp.where(keep, ablk, 0.0)
        return carry

    lax.fori_loop(0, chunk // sub, diag_block, 0)

    o = inter + _dot(a_ref[...].astype(BF16), v)

    k_out = (kf * jnp.exp(cum_end - cum)).astype(BF16)
    s_ref[...] = s_old * jnp.exp(cum_end) + _dot_tn(v, k_out)

    if rev:
        tot = ofw_ref[...] + o
        ms = jnp.mean(tot * tot, axis=-1, keepdims=True)
        y = tot * lax.rsqrt(ms + eps) * hn_ref[...]
        r = r_ref[...].astype(F32)
        o_ref[...] = (y * r * jax.nn.sigmoid(r)).astype(o_ref.dtype)
    else:
        o_ref[...] = o


def _gla_constants(cfg, rev):
    c, sub = cfg.gla_chunk, cfg.gla_sub
    i = jnp.arange(c)[:, None]
    j = jnp.arange(c)[None, :]
    tri = ((j >= i) if rev else (j <= i)).astype(BF16)
    masks = []
    s = c // 2
    while s >= sub:
        masks.append((i // (2 * s) == j // (2 * s)).astype(F32))
        s //= 2
    return tri, jnp.stack(masks)


def _gla(cfg, u, low, gate_b, gate_bias, rev, o_fw=None, head_norm=None):
    m = u.shape[0]
    hh, dk, dv, c = cfg.gla_heads, cfg.gla_dk, cfg.gla_dv, cfg.gla_chunk
    rank = cfg.gla_rank
    n_chunks = cfg.lp // c
    tri, lmask = _gla_constants(cfg, rev)
    n_lvl = lmask.shape[0]
    v_col0 = 2 * hh * dk // dv
    r_col0 = v_col0 + hh

    def rowblk(b, h, ci):
        return b * n_chunks + ((n_chunks - 1 - ci) if rev else ci)

    in_specs = [
        pl.BlockSpec((c, dk), lambda b, h, ci: (rowblk(b, h, ci), h)),
        pl.BlockSpec((c, dk), lambda b, h, ci: (rowblk(b, h, ci), hh + h)),
        pl.BlockSpec((c, dv), lambda b, h, ci: (rowblk(b, h, ci), v_col0 + h)),
        pl.BlockSpec((c, rank), lambda b, h, ci: (rowblk(b, h, ci), 0)),
        pl.BlockSpec((rank, dk), lambda b, h, ci: (0, h)),
        pl.BlockSpec((1, dk), lambda b, h, ci: (0, h)),
        pl.BlockSpec((c, c), lambda b, h, ci: (0, 0)),
        pl.BlockSpec((n_lvl, c, c), lambda b, h, ci: (0, 0, 0)),
    ]
    args = [u, u, u, low, gate_b, gate_bias.reshape(1, -1), tri, lmask]
    if rev:
        in_specs += [
            pl.BlockSpec((c, dv), lambda b, h, ci: (rowblk(b, h, ci), h)),
            pl.BlockSpec((c, dv), lambda b, h, ci: (rowblk(b, h, ci), r_col0 + h)),
            pl.BlockSpec((1, dv), lambda b, h, ci: (0, 0)),
        ]
        args += [o_fw, u, head_norm.reshape(1, dv)]
    kern = functools.partial(_gla_kernel, rev=rev, chunk=c, sub=cfg.gla_sub,
                             n_chunks=n_chunks, n_pad=cfg.n_pad,
                             tau=GLA_GATE_TAU, eps=EPS)
    return pl.pallas_call(
        kern,
        out_shape=jax.ShapeDtypeStruct((m, hh * dv), BF16 if rev else F32),
        grid=(cfg.batch, hh, n_chunks),
        in_specs=in_specs,
        out_specs=pl.BlockSpec((c, dv), lambda b, h, ci: (rowblk(b, h, ci), h)),
        scratch_shapes=[pltpu.VMEM((dv, dk), F32), pltpu.VMEM((c, dk), F32),
                        pltpu.VMEM((c, dk), F32), pltpu.VMEM((c, dk), F32),
                        pltpu.VMEM((c, c), F32)],
        compiler_params=_params("parallel", "parallel", "arbitrary"),
        name="gla_reverse" if rev else "gla_forward",
    )(*args)


def _rope_tables(cfg):
    hd = cfg.head_dim
    n_freq = hd // 4
    t = jnp.arange(cfg.seq)
    rows = (t // cfg.grid_w).astype(F32)
    cols = (t % cfg.grid_w).astype(F32)
    inv_freq = jnp.power(ROPE_THETA, -jnp.arange(n_freq, dtype=F32) / n_freq)
    ang = jnp.concatenate([jnp.tile(rows[:, None] * inv_freq, (1, 2)),
                           jnp.tile(cols[:, None] * inv_freq, (1, 2))], axis=1)
    ang = jnp.concatenate([jnp.zeros((cfg.n_pad + cfg.n_meta, hd), F32), ang], axis=0)
    sign = jnp.where((jnp.arange(hd) % (hd // 2)) < n_freq, -1.0, 1.0).astype(F32)
    cos_t = jnp.tile(jnp.cos(ang), (cfg.batch, 1))
    sin_t = jnp.tile(jnp.sin(ang) * sign, (cfg.batch, 1))
    return cos_t, sin_t


def _dft_tables(cfg):
    ll, gd, lp, n_pad = cfg.length, cfg.four_dim, cfg.lp, cfg.n_pad
    scale = 1.0 / math.sqrt(ll * gd)
    n_hi = -(-ll // LANES)
    t = jnp.arange(ll, dtype=jnp.int32)[None, :]
    k_lo = jnp.arange(LANES, dtype=jnp.int32)[:, None]
    k_hi = jnp.arange(n_hi, dtype=jnp.int32)[:, None] * LANES
    ang_lo = (2.0 * math.pi / ll) * ((k_lo * t) % ll).astype(F32)
    ang_hi = (2.0 * math.pi / ll) * ((k_hi * t) % ll).astype(F32)
    c_lo, s_lo = jnp.cos(ang_lo)[None], jnp.sin(ang_lo)[None]
    c_hi, s_hi = (jnp.cos(ang_hi) * scale)[:, None], (jnp.sin(ang_hi) * scale)[:, None]
    wc = (c_hi * c_lo - s_hi * s_lo).reshape(n_hi * LANES, ll)[:ll]
    ws = (s_hi * c_lo + c_hi * s_lo).reshape(n_hi * LANES, ll)[:ll]
    pad = ((n_pad, lp - n_pad - ll), (n_pad, lp - n_pad - ll))
    wc = jnp.pad(wc.astype(BF16), pad)
    ws = jnp.pad(ws.astype(BF16), pad)
    cc = jnp.arange(gd, dtype=jnp.int32)
    ang_c = (2.0 * math.pi / gd) * ((cc[:, None] * cc[None, :]) % gd).astype(F32)
    cs = jnp.concatenate([jnp.cos(ang_c), jnp.sin(ang_c)], axis=1).astype(BF16)
    return wc, ws, cs


def _forward(cfg, x, meta_tokens, pre_norm, ffn_w_gate, ffn_w_up, ffn_w_down, even_w_in,
             even_q_norm, even_k_norm, even_w_out, odd_w_in, odd_gate_a, odd_gate_b,
             odd_gate_bias, odd_head_norm, odd_w_out):
    bsz, d = cfg.batch, cfg.d_model
    meta = jnp.broadcast_to(meta_tokens.astype(F32)[None], (bsz, cfg.n_meta, d))
    h = jnp.concatenate([jnp.zeros((bsz, cfg.n_pad, d), F32), meta, x.astype(F32)], axis=1)
    h = h.reshape(cfg.rows, d)

    wg, wu, wd = ffn_w_gate.astype(BF16), ffn_w_up.astype(BF16), ffn_w_down.astype(BF16)
    ew_in, ew_out = even_w_in.astype(BF16), even_w_out.astype(BF16)
    ow_in, ow_out = odd_w_in.astype(BF16), odd_w_out.astype(BF16)
    ga = jnp.concatenate([odd_gate_a[:, 0], odd_gate_a[:, 1]], axis=-1).astype(BF16)

    cos_t, sin_t = _rope_tables(cfg)
    wc, ws, cs = _dft_tables(cfg)

    def ffn(h, layer, slot, norm_slot):
        hn = _rmsnorm(cfg, h, pre_norm[layer, norm_slot])
        a = _ffn_up(cfg, hn, wg, wu, layer, slot)
        return _mm_res(cfg, [a], wd, (layer, slot), h, 0.5)

    for layer in range(cfg.depth):
        i = layer // 2
        h = ffn(h, layer, 0, 0)
        hn = _rmsnorm(cfg, h, pre_norm[layer, 1])
        if layer % 2 == 0:
            u = _even_inproj(cfg, hn, ew_in, i, cos_t, sin_t, even_q_norm[i], even_k_norm[i])
            attn = _attention(cfg, u)
            fa, fb = _chan_dft(cfg, u, cs)
            four = _seq_dft(cfg, wc, ws, fa, fb)
            h = _mm_res(cfg, [attn, four], ew_out, (i,), h, 1.0)
        else:
            u = _odd_inproj(cfg, hn, ow_in, i)
            low = _gate_low(cfg, hn, ga[i])
            rank = cfg.gla_rank
            o_fw = _gla(cfg, u, low[:, :rank], odd_gate_b[i, 0], odd_gate_bias[i, 0], rev=False)
            og = _gla(cfg, u, low[:, rank:], odd_gate_b[i, 1], odd_gate_bias[i, 1], rev=True,
                      o_fw=o_fw, head_norm=odd_head_norm[i])
            h = _mm_res(cfg, [og], ow_out, (i,), h, 1.0)
        h = ffn(h, layer, 1, 2)
    return h.reshape(bsz, cfg.lp, d)[:, cfg.n_pad + cfg.n_meta:]


def kernel(x, meta_tokens, pre_norm, ffn_w_gate, ffn_w_up, ffn_w_down, even_w_in, even_q_norm,
           even_k_norm, even_w_out, odd_w_in, odd_gate_a, odd_gate_b, odd_gate_bias,
           odd_head_norm, odd_w_out):
    bsz, seq, d = x.shape
    cfg = Cfg(batch=bsz, seq=seq, d_model=d, d_ff=ffn_w_gate.shape[-1], depth=pre_norm.shape[0],
              n_meta=N_META, grid_w=GRID_W, attn_heads=ATTN_HEADS, attn_kv_heads=ATTN_KV_HEADS,
              head_dim=HEAD_DIM, q_block=Q_BLOCK, four_groups=FOURIER_GROUPS,
              four_dim=FOURIER_GROUP_DIM, gla_heads=GLA_HEADS, gla_rank=odd_gate_a.shape[-1],
              tm_norm=320, tm=640, tn=512, attn_tk=640, dft_tm=1664, dft_tn=1024, dft_tk=640,
              gla_chunk=128, gla_sub=16)
    return _forward(cfg, x, meta_tokens, pre_norm, ffn_w_gate, ffn_w_up, ffn_w_down, even_w_in,
                    even_q_norm, even_k_norm, even_w_out, odd_w_in, odd_gate_a, odd_gate_b,
                    odd_gate_bias, odd_head_norm, odd_w_out)
```

```python
import functools
import math
from typing import NamedTuple

import jax
import jax.numpy as jnp
from jax import lax
from jax.experimental import pallas as pl
from jax.experimental.pallas import tpu as pltpu

F32 = jnp.float32
BF16 = jnp.bfloat16

N_META = 16
GRID_W = 64
EPS = 1e-6
ATTN_HEADS = 16
ATTN_KV_HEADS = 4
HEAD_DIM = 128
Q_BLOCK = 128
ROPE_THETA = 10000.0
FOURIER_GROUPS = 16
FOURIER_GROUP_DIM = 128
GLA_HEADS = 8
GLA_GATE_TAU = 16.0

V7X_VMEM_BYTES = 64 * 1024 * 1024
VMEM_LIMIT_BYTES = V7X_VMEM_BYTES - 8 * 1024 * 1024
LANES = 128
SUBLANES = 8
NEG_BIG = -1e30


class Cfg(NamedTuple):
    batch: int
    seq: int
    d_model: int
    d_ff: int
    depth: int
    n_meta: int
    grid_w: int
    attn_heads: int
    attn_kv_heads: int
    head_dim: int
    q_block: int
    four_groups: int
    four_dim: int
    gla_heads: int
    gla_rank: int
    tm_norm: int
    tm: int
    tn: int
    attn_tk: int
    attn_unroll: int
    dft_tm: int
    dft_tn: int
    dft_tk: int
    gla_chunk: int
    gla_sub: int
    gla_heads_per_step: int

    @property
    def length(self):
        return self.n_meta + self.seq

    @property
    def n_pad(self):
        return (-self.length) % self.q_block

    @property
    def lp(self):
        return self.length + self.n_pad

    @property
    def rows(self):
        return self.batch * self.lp

    @property
    def q_w(self):
        return self.attn_heads * self.head_dim

    @property
    def kv_w(self):
        return self.attn_kv_heads * self.head_dim

    @property
    def four_w(self):
        return self.four_groups * self.four_dim

    @property
    def gla_dk(self):
        return self.d_model // (2 * self.gla_heads)

    @property
    def gla_dv(self):
        return self.d_model // self.gla_heads


def _params(*sem):
    return pltpu.CompilerParams(dimension_semantics=sem, vmem_limit_bytes=VMEM_LIMIT_BYTES)


def _dot(a, b):
    return jnp.dot(a, b, preferred_element_type=F32)


def _dot_nt(a, b):
    return lax.dot_general(a, b, (((1,), (1,)), ((), ())), preferred_element_type=F32)


def _dot_tn(a, b):
    return lax.dot_general(a, b, (((0,), (0,)), ((), ())), preferred_element_type=F32)


def _rmsnorm_kernel(x_ref, g_ref, o_ref, *, eps):
    x = x_ref[...]
    ms = jnp.mean(x * x, axis=-1, keepdims=True)
    o_ref[...] = (x * lax.rsqrt(ms + eps) * g_ref[...]).astype(o_ref.dtype)


def _rmsnorm(cfg, h, gain):
    m, d = h.shape
    tm = cfg.tm_norm
    return pl.pallas_call(
        functools.partial(_rmsnorm_kernel, eps=EPS),
        out_shape=jax.ShapeDtypeStruct((m, d), BF16),
        grid=(m // tm,),
        in_specs=[pl.BlockSpec((tm, d), lambda i: (i, 0)),
                  pl.BlockSpec((1, d), lambda i: (0, 0))],
        out_specs=pl.BlockSpec((tm, d), lambda i: (i, 0)),
        compiler_params=_params("parallel"),
        name="rmsnorm",
    )(h, gain.reshape(1, d))


def _ffn_up_kernel(x_ref, wg_ref, wu_ref, o_ref):
    x = x_ref[...]
    g = _dot(x, wg_ref[...])
    u = _dot(x, wu_ref[...])
    o_ref[...] = (g * jax.nn.sigmoid(g) * u).astype(o_ref.dtype)


def _ffn_up(cfg, x, wg, wu, layer, slot):
    m, d = x.shape
    f = wg.shape[-1]
    tm, tn = cfg.tm, cfg.tn
    wspec = pl.BlockSpec((None, None, d, tn), lambda i, j: (layer, slot, 0, j))
    return pl.pallas_call(
        _ffn_up_kernel,
        out_shape=jax.ShapeDtypeStruct((m, f), BF16),
        grid=(m // tm, f // tn),
        in_specs=[pl.BlockSpec((tm, d), lambda i, j: (i, 0)), wspec, wspec],
        out_specs=pl.BlockSpec((tm, tn), lambda i, j: (i, j)),
        compiler_params=_params("parallel", "arbitrary"),
        name="ffn_up",
    )(x, wg, wu)


def _mm_res_kernel(*refs, n_lhs, scale):
    xs, ws = refs[:n_lhs], refs[n_lhs:2 * n_lhs]
    h_ref, o_ref = refs[2 * n_lhs], refs[2 * n_lhs + 1]
    acc = _dot(xs[0][...], ws[0][...])
    for x_ref, w_ref in zip(xs[1:], ws[1:]):
        acc = acc + _dot(x_ref[...], w_ref[...])
    o_ref[...] = h_ref[...] + scale * acc


def _mm_res(cfg, xs, w, w_lead, h, scale):
    m, d = h.shape
    kx = xs[0].shape[1]
    n_lhs = len(xs)
    tm, tn = cfg.tm, cfg.tn
    n_lead = len(w_lead)
    w_specs = [
        pl.BlockSpec((None,) * n_lead + (kx, tn), lambda i, j, p=p: tuple(w_lead) + (p, j))
        for p in range(n_lhs)
    ]
    return pl.pallas_call(
        functools.partial(_mm_res_kernel, n_lhs=n_lhs, scale=scale),
        out_shape=jax.ShapeDtypeStruct((m, d), F32),
        grid=(m // tm, d // tn),
        in_specs=[pl.BlockSpec((tm, kx), lambda i, j: (i, 0))] * n_lhs + w_specs
        + [pl.BlockSpec((tm, tn), lambda i, j: (i, j))],
        out_specs=pl.BlockSpec((tm, tn), lambda i, j: (i, j)),
        input_output_aliases={2 * n_lhs: 0},
        compiler_params=_params("parallel", "arbitrary"),
        name="mm_residual",
    )(*xs, *([w] * n_lhs), h)


def _even_inproj_kernel(x_ref, w_ref, cos_ref, sin_ref, qg_ref, kg_ref, o_ref, qt_ref, vt_ref, *,
                        n_q_tiles, n_k_tiles, hd, eps, q_scale):
    j = pl.program_id(1)
    acc = _dot(x_ref[...], w_ref[...])
    heads_per_tile = acc.shape[1] // hd

    def normed_rotated(gain):
        c = cos_ref[...]
        s = sin_ref[...]
        lane = lax.broadcasted_iota(jnp.int32, c.shape, 1)
        first_half = (lane % (hd // 2)) < (hd // 4)
        for hh in range(heads_per_tile):
            xh = acc[:, hh * hd:(hh + 1) * hd]
            ms = jnp.mean(xh * xh, axis=-1, keepdims=True)
            y = xh * lax.rsqrt(ms + eps) * gain
            partner = jnp.where(first_half,
                                pltpu.roll(y, hd - hd // 4, axis=1),
                                pltpu.roll(y, hd // 4, axis=1))
            yield hh, y * c + partner * s

    @pl.when(j < n_q_tiles)
    def _():
        for hh, y in normed_rotated(qg_ref[...] * q_scale):
            o_ref[:, hh * hd:(hh + 1) * hd] = y.astype(o_ref.dtype)
            qt_ref[hh * hd:(hh + 1) * hd, :] = y.T.astype(qt_ref.dtype)

    @pl.when((j >= n_q_tiles) & (j < n_q_tiles + n_k_tiles))
    def _():
        for hh, y in normed_rotated(kg_ref[...]):
            o_ref[:, hh * hd:(hh + 1) * hd] = y.astype(o_ref.dtype)

    @pl.when(j >= n_q_tiles + n_k_tiles)
    def _():
        o_ref[...] = acc.astype(o_ref.dtype)

    @pl.when((j >= n_q_tiles + n_k_tiles) & (j < n_q_tiles + 2 * n_k_tiles))
    def _():
        vt_ref[...] = acc.T.astype(vt_ref.dtype)


def _even_inproj(cfg, x, w, layer_i, cos_t, sin_t, q_gain, k_gain):
    m, d = x.shape
    n = w.shape[-1]
    tm, tn, hd = cfg.tm, cfg.tn, cfg.head_dim
    assert cfg.q_w % tn == 0 and cfg.kv_w % tn == 0 and tn % hd == 0
    n_q, n_k = cfg.q_w // tn, cfg.kv_w // tn
    kern = functools.partial(
        _even_inproj_kernel, n_q_tiles=n_q, n_k_tiles=n_k, hd=hd,
        eps=EPS, q_scale=hd ** -0.5 * math.log2(math.e))
    return pl.pallas_call(
        kern,
        out_shape=(jax.ShapeDtypeStruct((m, n), BF16),
                   jax.ShapeDtypeStruct((cfg.q_w, m), BF16),
                   jax.ShapeDtypeStruct((cfg.kv_w, m), BF16)),
        grid=(m // tm, n // tn),
        in_specs=[pl.BlockSpec((tm, d), lambda i, j: (i, 0)),
                  pl.BlockSpec((None, d, tn), lambda i, j: (layer_i, 0, j)),
                  pl.BlockSpec((tm, hd), lambda i, j: (i, 0)),
                  pl.BlockSpec((tm, hd), lambda i, j: (i, 0)),
                  pl.BlockSpec((1, hd), lambda i, j: (0, 0)),
                  pl.BlockSpec((1, hd), lambda i, j: (0, 0))],
        out_specs=(
            pl.BlockSpec((tm, tn), lambda i, j: (i, j)),
            pl.BlockSpec((tn, tm), lambda i, j: (jnp.minimum(j, n_q - 1), i)),
            pl.BlockSpec((tn, tm), lambda i, j: (jnp.clip(j - n_q - n_k, 0, n_k - 1), i))),
        compiler_params=_params("parallel", "arbitrary"),
        name="even_inproj",
    )(x, w, cos_t, sin_t, q_gain.reshape(1, hd), k_gain.reshape(1, hd))


def _attn_kernel(qt_ref, k_ref, vt_ref, o_ref, acc_ref, *, groups, hpc, hd, sub, n_sub, n_pad,
                 unroll):
    tq = qt_ref.shape[1]
    n_chain = groups // hpc
    cw = hpc * tq
    q_t = [jnp.concatenate([qt_ref[(c * hpc + g) * hd:(c * hpc + g + 1) * hd, :]
                            for g in range(hpc)], axis=1) for c in range(n_chain)]

    def scores(j):
        k_blk = k_ref[pl.ds(pl.multiple_of(j * sub, sub), sub), :]
        return tuple(_dot(k_blk, q_t[c]) for c in range(n_chain))

    def consume(j, st, ml, masked):
        v_blk = vt_ref[:, pl.ds(pl.multiple_of(j * sub, sub), sub)]
        out = []
        for c in range(n_chain):
            m_prev, l_prev = ml[c]
            s = st[c]
            if masked:
                row = lax.broadcasted_iota(jnp.int32, s.shape, 0)
                s = jnp.where(row < n_pad, NEG_BIG, s)
            m_new = jnp.maximum(m_prev, jnp.max(s, axis=0, keepdims=True))
            alpha = jnp.exp2(m_prev - m_new)
            p = jnp.exp2(s - m_new)
            l_new = alpha * l_prev + jnp.sum(p, axis=0, keepdims=True)
            acc_ref[c] = alpha * acc_ref[c] + _dot(v_blk, p.astype(v_blk.dtype))
            out.append((m_new, l_new))
        return tuple(out)

    acc_ref[...] = jnp.zeros_like(acc_ref)
    ml = tuple((jnp.full((1, cw), NEG_BIG, F32), jnp.zeros((1, cw), F32))
               for _ in range(n_chain))
    st = scores(0)
    st_next = scores(1)
    ml = consume(0, st, ml, True)

    def body(it, carry):
        st, ml = carry
        for u in range(unroll):
            j = 1 + it * unroll + u
            st_next = scores(jnp.minimum(j + 1, n_sub - 1))
            ml = consume(j, st, ml, False)
            st = st_next
        return st, ml

    _, ml = lax.fori_loop(0, (n_sub - 1) // unroll, body, (st_next, ml))
    for c in range(n_chain):
        o_t = acc_ref[c] / ml[c][1]
        for g in range(hpc):
            h0 = (c * hpc + g) * hd
            o_ref[:, h0:h0 + hd] = o_t[:, g * tq:(g + 1) * tq].T.astype(o_ref.dtype)


def _attention(cfg, u, qt, vt):
    m = u.shape[0]
    hd, kv, tq, lp = cfg.head_dim, cfg.attn_kv_heads, cfg.q_block, cfg.lp
    groups = cfg.attn_heads // kv
    hpc = min(groups, 2)
    n_q = lp // tq
    sub = cfg.attn_tk
    n_sub = lp // sub
    assert cfg.n_pad <= sub and lp % sub == 0 and (n_sub - 1) % cfg.attn_unroll == 0
    kern = functools.partial(_attn_kernel, groups=groups, hpc=hpc, hd=hd, sub=sub,
                             n_sub=n_sub, n_pad=cfg.n_pad, unroll=cfg.attn_unroll)
    k_col0 = cfg.q_w // hd
    return pl.pallas_call(
        kern,
        out_shape=jax.ShapeDtypeStruct((m, cfg.q_w), BF16),
        grid=(cfg.batch, kv, n_q),
        in_specs=[pl.BlockSpec((groups * hd, tq), lambda b, h, i: (h, b * n_q + i)),
                  pl.BlockSpec((lp, hd), lambda b, h, i: (b, k_col0 + h)),
                  pl.BlockSpec((hd, lp), lambda b, h, i: (h, b))],
        out_specs=pl.BlockSpec((tq, groups * hd), lambda b, h, i: (b * n_q + i, h)),
        scratch_shapes=[pltpu.VMEM((groups // hpc, hd, hpc * tq), F32)],
        compiler_params=_params("parallel", "parallel", "arbitrary"),
        name="gqa_attention",
    )(qt, u, vt)


def _chan_dft_kernel(f_ref, cs_ref, a_ref, b_ref, *, gd):
    cs = cs_ref[...]
    for g in range(f_ref.shape[1] // gd):
        ab = _dot(f_ref[:, g * gd:(g + 1) * gd], cs)
        a_ref[:, g * gd:(g + 1) * gd] = ab[:, :gd].astype(a_ref.dtype)
        b_ref[:, g * gd:(g + 1) * gd] = ab[:, gd:].astype(b_ref.dtype)


def _chan_dft(cfg, u, cs):
    m = u.shape[0]
    gd, fw = cfg.four_dim, cfg.four_w
    f0 = cfg.q_w + 2 * cfg.kv_w
    fb = math.gcd(math.gcd(f0, fw), 1024)
    tm = cfg.tm
    out = jax.ShapeDtypeStruct((m, fw), BF16)
    return pl.pallas_call(
        functools.partial(_chan_dft_kernel, gd=gd),
        out_shape=(out, out),
        grid=(m // tm, fw // fb),
        in_specs=[pl.BlockSpec((tm, fb), lambda i, j: (i, f0 // fb + j)),
                  pl.BlockSpec((gd, 2 * gd), lambda i, j: (0, 0))],
        out_specs=(pl.BlockSpec((tm, fb), lambda i, j: (i, j)),
                   pl.BlockSpec((tm, fb), lambda i, j: (i, j))),
        compiler_params=_params("parallel", "parallel"),
        name="fourier_channel_dft",
    )(u, cs)


def _seq_dft_kernel(wc_ref, ws_ref, a_ref, b_ref, o_ref, acc_ref):
    k = pl.program_id(3)

    @pl.when(k == 0)
    def _():
        acc_ref[...] = jnp.zeros_like(acc_ref)

    acc_ref[...] += _dot(wc_ref[...], a_ref[...]) - _dot(ws_ref[...], b_ref[...])

    @pl.when(k == pl.num_programs(3) - 1)
    def _():
        o_ref[...] = acc_ref[...].astype(o_ref.dtype)


def _seq_dft(cfg, wc, ws, a, b):
    m, fw = a.shape
    lp = cfg.lp
    tm, tn, tk = cfg.dft_tm, cfg.dft_tn, cfg.dft_tk
    n_m, n_k = lp // tm, lp // tk
    return pl.pallas_call(
        _seq_dft_kernel,
        out_shape=jax.ShapeDtypeStruct((m, fw), BF16),
        grid=(cfg.batch, fw // tn, n_m, n_k),
        in_specs=[pl.BlockSpec((tm, tk), lambda bb, n, i, k: (i, k)),
                  pl.BlockSpec((tm, tk), lambda bb, n, i, k: (i, k)),
                  pl.BlockSpec((tk, tn), lambda bb, n, i, k: (bb * n_k + k, n)),
                  pl.BlockSpec((tk, tn), lambda bb, n, i, k: (bb * n_k + k, n))],
        out_specs=pl.BlockSpec((tm, tn), lambda bb, n, i, k: (bb * n_m + i, n)),
        scratch_shapes=[pltpu.VMEM((tm, tn), F32)],
        compiler_params=_params("parallel", "parallel", "parallel", "arbitrary"),
        name="fourier_sequence_dft",
    )(wc, ws, a, b)


def _odd_inproj_kernel(x_ref, w_ref, o_ref, *, n_q_tiles, q_scale):
    acc = _dot(x_ref[...], w_ref[...])
    sc = jnp.where(pl.program_id(1) < n_q_tiles, q_scale, 1.0).astype(F32)
    o_ref[...] = (acc * sc).astype(o_ref.dtype)


def _odd_inproj(cfg, x, w, layer_i):
    m, d = x.shape
    n = w.shape[-1]
    tm, tn = cfg.tm, cfg.tn
    kern = functools.partial(_odd_inproj_kernel, n_q_tiles=cfg.gla_heads * cfg.gla_dk // tn,
                             q_scale=cfg.gla_dk ** -0.5)
    return pl.pallas_call(
        kern,
        out_shape=jax.ShapeDtypeStruct((m, n), BF16),
        grid=(m // tm, n // tn),
        in_specs=[pl.BlockSpec((tm, d), lambda i, j: (i, 0)),
                  pl.BlockSpec((None, d, tn), lambda i, j: (layer_i, 0, j))],
        out_specs=pl.BlockSpec((tm, tn), lambda i, j: (i, j)),
        compiler_params=_params("parallel", "arbitrary"),
        name="odd_inproj",
    )(x, w)


def _gate_low_kernel(x_ref, w_ref, o_ref):
    o_ref[...] = _dot(x_ref[...], w_ref[...])


def _gate_low(cfg, x, ga):
    m, d = x.shape
    r2 = ga.shape[1]
    tm = cfg.tm
    return pl.pallas_call(
        _gate_low_kernel,
        out_shape=jax.ShapeDtypeStruct((m, r2), F32),
        grid=(m // tm,),
        in_specs=[pl.BlockSpec((tm, d), lambda i: (i, 0)),
                  pl.BlockSpec((d, r2), lambda i: (0, 0))],
        out_specs=pl.BlockSpec((tm, r2), lambda i: (i, 0)),
        compiler_params=_params("parallel"),
        name="gla_gate_lowrank",
    )(x, ga)


def _gla_kernel(*refs, rev, chunk, sub, halo, hpb, dk, dv, n_chunks, n_pad, tau, eps):
    if rev:
        (q_ref, k_ref, v_ref, low_ref, gb_ref, bias_ref, tri_ref, lmask_ref,
         ofw_ref, r_ref, hn_ref, o_ref, s_ref, cump_ref, kfp_ref) = refs
    else:
        (q_ref, k_ref, v_ref, low_ref, gb_ref, bias_ref, tri_ref, lmask_ref,
         o_ref, s_ref, cump_ref, kfp_ref) = refs
    c = pl.program_id(2)
    pos_chunk = (n_chunks - 1 - c) if rev else c

    @pl.when(c == 0)
    def _():
        s_ref[...] = jnp.zeros_like(s_ref)
        cump_ref[...] = jnp.zeros_like(cump_ref)
        kfp_ref[...] = jnp.zeros_like(kfp_ref)

    row = pos_chunk * chunk + lax.broadcasted_iota(jnp.int32, (chunk, 1), 0)
    valid = row >= n_pad
    col_minus_row = (lax.broadcasted_iota(jnp.int32, (chunk, chunk), 1)
                     - lax.broadcasted_iota(jnp.int32, (chunk, chunk), 0))
    row_in_sub = lax.broadcasted_iota(jnp.int32, (chunk, 1), 0) % sub
    low = low_ref[...]
    tri = tri_ref[...]

    def head(hx):
        ks = slice(hx * dk, (hx + 1) * dk)
        vs = slice(hx * dv, (hx + 1) * dv)
        z = jnp.dot(low, gb_ref[:, ks], preferred_element_type=F32,
                    precision=lax.Precision.HIGHEST) + bias_ref[:, ks]
        log_a = (jnp.minimum(z, 0.0) - jnp.log(1.0 + jnp.exp(-jnp.abs(z)))) * (1.0 / tau)
        log_a = jnp.where(valid, log_a, 0.0)
        kf = jnp.where(valid, k_ref[:, ks].astype(F32), 0.0)
        qf = q_ref[:, ks].astype(F32)
        v = v_ref[:, vs]

        g_hi = log_a.astype(BF16)
        g_lo = (log_a - g_hi.astype(F32)).astype(BF16)
        cum = _dot(tri, g_hi) + _dot(tri, g_lo)
        cum_end = cum[0:1] if rev else cum[chunk - 1:chunk]
        yield

        s_old = s_ref[hx]
        inter = _dot_nt((qf * jnp.exp(cum)).astype(BF16), s_old.astype(BF16))
        yield

        a_acc = jnp.zeros((chunk, chunk), F32)
        s_blk, level = chunk // 2, 0
        while s_blk >= sub:
            grp = 2 * s_blk
            ref_idx = s_blk if rev else s_blk - 1
            ref = jnp.broadcast_to(
                cum.reshape(chunk // grp, grp, dk)[:, ref_idx:ref_idx + 1, :],
                (chunk // grp, grp, dk)).reshape(chunk, dk)
            e = jnp.exp(-jnp.abs(cum - ref))
            a_lvl = _dot_nt((qf * e).astype(BF16), (kf * e).astype(BF16))
            a_acc = a_acc + a_lvl * lmask_ref[level]
            s_blk //= 2
            level += 1
            yield

        cump_ref[hx, halo:halo + chunk, :] = cum
        kfp_ref[hx, halo:halo + chunk, :] = kf
        a_acc = a_acc + jnp.where(col_minus_row == 0,
                                  jnp.sum(qf * kf, axis=-1, keepdims=True), 0.0)
        for delta in range(1, sub):
            off = halo + delta if rev else halo - delta
            k_sh = kfp_ref[hx, off:off + chunk, :]
            cum_sh = cump_ref[hx, off:off + chunk, :]
            col = jnp.sum(qf * k_sh * jnp.exp(cum - cum_sh), axis=-1, keepdims=True)
            same_sub = (row_in_sub + delta < sub) if rev else (row_in_sub >= delta)
            col = jnp.where(same_sub, col, 0.0)
            a_acc = a_acc + jnp.where(col_minus_row == (delta if rev else -delta), col, 0.0)
            yield

        o = inter + _dot(a_acc.astype(BF16), v)

        k_out = (kf * jnp.exp(cum_end - cum)).astype(BF16)
        s_ref[hx] = s_old * jnp.exp(cum_end) + _dot_tn(v, k_out)
        yield

        if rev:
            tot = ofw_ref[:, vs] + o
            ms = jnp.mean(tot * tot, axis=-1, keepdims=True)
            y = tot * lax.rsqrt(ms + eps) * hn_ref[...]
            r = r_ref[:, vs].astype(F32)
            o_ref[:, vs] = (y * r * jax.nn.sigmoid(r)).astype(o_ref.dtype)
        else:
            o_ref[:, vs] = o

    chains = [head(hx) for hx in range(hpb)]
    while chains:
        for g in list(chains):
            if next(g, "done") == "done":
                chains.remove(g)


def _gla_constants(cfg, rev):
    c, sub = cfg.gla_chunk, cfg.gla_sub
    i = jnp.arange(c)[:, None]
    j = jnp.arange(c)[None, :]
    tri = ((j >= i) if rev else (j <= i)).astype(BF16)
    masks = []
    s = c // 2
    while s >= sub:
        i_hi, j_hi = (i // s) % 2 == 1, (j // s) % 2 == 1
        roles = (~i_hi & j_hi) if rev else (i_hi & ~j_hi)
        masks.append(((i // (2 * s) == j // (2 * s)) & roles).astype(F32))
        s //= 2
    return tri, jnp.stack(masks)


def _gla(cfg, u, low, gate_b, gate_bias, rev, o_fw=None, head_norm=None):
    m = u.shape[0]
    hh, dk, dv, c = cfg.gla_heads, cfg.gla_dk, cfg.gla_dv, cfg.gla_chunk
    rank = cfg.gla_rank
    n_chunks = cfg.lp // c
    tri, lmask = _gla_constants(cfg, rev)
    n_lvl = lmask.shape[0]
    hpb = cfg.gla_heads_per_step
    kw, vw = hpb * dk, hpb * dv
    assert hh % hpb == 0 and (2 * hh * dk) % vw == 0
    k_col0 = hh // hpb
    v_col0 = 2 * hh * dk // vw
    r_col0 = v_col0 + hh // hpb

    def rowblk(b, h, ci):
        return b * n_chunks + ((n_chunks - 1 - ci) if rev else ci)

    in_specs = [
        pl.BlockSpec((c, kw), lambda b, h, ci: (rowblk(b, h, ci), h)),
        pl.BlockSpec((c, kw), lambda b, h, ci: (rowblk(b, h, ci), k_col0 + h)),
        pl.BlockSpec((c, vw), lambda b, h, ci: (rowblk(b, h, ci), v_col0 + h)),
        pl.BlockSpec((c, rank), lambda b, h, ci: (rowblk(b, h, ci), 0)),
        pl.BlockSpec((rank, kw), lambda b, h, ci: (0, h)),
        pl.BlockSpec((1, kw), lambda b, h, ci: (0, h)),
        pl.BlockSpec((c, c), lambda b, h, ci: (0, 0)),
        pl.BlockSpec((n_lvl, c, c), lambda b, h, ci: (0, 0, 0)),
    ]
    args = [u, u, u, low, gate_b, gate_bias.reshape(1, -1), tri, lmask]
    if rev:
        in_specs += [
            pl.BlockSpec((c, vw), lambda b, h, ci: (rowblk(b, h, ci), h)),
            pl.BlockSpec((c, vw), lambda b, h, ci: (rowblk(b, h, ci), r_col0 + h)),
            pl.BlockSpec((1, dv), lambda b, h, ci: (0, 0)),
        ]
        args += [o_fw, u, head_norm.reshape(1, dv)]
    halo = -(-cfg.gla_sub // SUBLANES) * SUBLANES
    kern = functools.partial(_gla_kernel, rev=rev, chunk=c, sub=cfg.gla_sub, halo=halo,
                             hpb=hpb, dk=dk, dv=dv, n_chunks=n_chunks, n_pad=cfg.n_pad,
                             tau=GLA_GATE_TAU, eps=EPS)
    return pl.pallas_call(
        kern,
        out_shape=jax.ShapeDtypeStruct((m, hh * dv), BF16 if rev else F32),
        grid=(cfg.batch, hh // hpb, n_chunks),
        in_specs=in_specs,
        out_specs=pl.BlockSpec((c, vw), lambda b, h, ci: (rowblk(b, h, ci), h)),
        scratch_shapes=[pltpu.VMEM((hpb, dv, dk), F32),
                        pltpu.VMEM((hpb, c + 2 * halo, dk), F32),
                        pltpu.VMEM((hpb, c + 2 * halo, dk), F32)],
        compiler_params=_params("parallel", "parallel", "arbitrary"),
        name="gla_reverse" if rev else "gla_forward",
    )(*args)


def _rope_tables(cfg):
    hd = cfg.head_dim
    n_freq = hd // 4
    t = jnp.arange(cfg.seq)
    rows = (t // cfg.grid_w).astype(F32)
    cols = (t % cfg.grid_w).astype(F32)
    inv_freq = jnp.power(ROPE_THETA, -jnp.arange(n_freq, dtype=F32) / n_freq)
    ang = jnp.concatenate([jnp.tile(rows[:, None] * inv_freq, (1, 2)),
                           jnp.tile(cols[:, None] * inv_freq, (1, 2))], axis=1)
    ang = jnp.concatenate([jnp.zeros((cfg.n_pad + cfg.n_meta, hd), F32), ang], axis=0)
    sign = jnp.where((jnp.arange(hd) % (hd // 2)) < n_freq, -1.0, 1.0).astype(F32)
    cos_t = jnp.tile(jnp.cos(ang), (cfg.batch, 1))
    sin_t = jnp.tile(jnp.sin(ang) * sign, (cfg.batch, 1))
    return cos_t, sin_t


def _dft_tables(cfg):
    ll, gd, lp, n_pad = cfg.length, cfg.four_dim, cfg.lp, cfg.n_pad
    scale = 1.0 / math.sqrt(ll * gd)
    n_hi = -(-ll // LANES)
    t = jnp.arange(ll, dtype=jnp.int32)[None, :]
    k_lo = jnp.arange(LANES, dtype=jnp.int32)[:, None]
    k_hi = jnp.arange(n_hi, dtype=jnp.int32)[:, None] * LANES
    ang_lo = (2.0 * math.pi / ll) * ((k_lo * t) % ll).astype(F32)
    ang_hi = (2.0 * math.pi / ll) * ((k_hi * t) % ll).astype(F32)
    c_lo, s_lo = jnp.cos(ang_lo)[None], jnp.sin(ang_lo)[None]
    c_hi, s_hi = (jnp.cos(ang_hi) * scale)[:, None], (jnp.sin(ang_hi) * scale)[:, None]
    wc = (c_hi * c_lo - s_hi * s_lo).reshape(n_hi * LANES, ll)[:ll]
    ws = (s_hi * c_lo + c_hi * s_lo).reshape(n_hi * LANES, ll)[:ll]
    pad = ((n_pad, lp - n_pad - ll), (n_pad, lp - n_pad - ll))
    wc = jnp.pad(wc.astype(BF16), pad)
    ws = jnp.pad(ws.astype(BF16), pad)
    cc = jnp.arange(gd, dtype=jnp.int32)
    ang_c = (2.0 * math.pi / gd) * ((cc[:, None] * cc[None, :]) % gd).astype(F32)
    cs = jnp.concatenate([jnp.cos(ang_c), jnp.sin(ang_c)], axis=1).astype(BF16)
    return wc, ws, cs


def _forward(cfg, x, meta_tokens, pre_norm, ffn_w_gate, ffn_w_up, ffn_w_down, even_w_in,
             even_q_norm, even_k_norm, even_w_out, odd_w_in, odd_gate_a, odd_gate_b,
             odd_gate_bias, odd_head_norm, odd_w_out):
    bsz, d = cfg.batch, cfg.d_model
    meta = jnp.broadcast_to(meta_tokens.astype(F32)[None], (bsz, cfg.n_meta, d))
    h = jnp.concatenate([jnp.zeros((bsz, cfg.n_pad, d), F32), meta, x.astype(F32)], axis=1)
    h = h.reshape(cfg.rows, d)

    wg, wu, wd = ffn_w_gate.astype(BF16), ffn_w_up.astype(BF16), ffn_w_down.astype(BF16)
    ew_in, ew_out = even_w_in.astype(BF16), even_w_out.astype(BF16)
    ow_in, ow_out = odd_w_in.astype(BF16), odd_w_out.astype(BF16)
    ga = jnp.concatenate([odd_gate_a[:, 0], odd_gate_a[:, 1]], axis=-1).astype(BF16)

    cos_t, sin_t = _rope_tables(cfg)
    wc, ws, cs = _dft_tables(cfg)

    def ffn(h, layer, slot, norm_slot):
        hn = _rmsnorm(cfg, h, pre_norm[layer, norm_slot])
        a = _ffn_up(cfg, hn, wg, wu, layer, slot)
        return _mm_res(cfg, [a], wd, (layer, slot), h, 0.5)

    for layer in range(cfg.depth):
        i = layer // 2
        h = ffn(h, layer, 0, 0)
        hn = _rmsnorm(cfg, h, pre_norm[layer, 1])
        if layer % 2 == 0:
            u, qt, vt = _even_inproj(cfg, hn, ew_in, i, cos_t, sin_t, even_q_norm[i],
                                     even_k_norm[i])
            attn = _attention(cfg, u, qt, vt)
            fa, fb = _chan_dft(cfg, u, cs)
            four = _seq_dft(cfg, wc, ws, fa, fb)
            h = _mm_res(cfg, [attn, four], ew_out, (i,), h, 1.0)
        else:
            u = _odd_inproj(cfg, hn, ow_in, i)
            low = _gate_low(cfg, hn, ga[i])
            rank = cfg.gla_rank
            o_fw = _gla(cfg, u, low[:, :rank], odd_gate_b[i, 0], odd_gate_bias[i, 0], rev=False)
            og = _gla(cfg, u, low[:, rank:], odd_gate_b[i, 1], odd_gate_bias[i, 1], rev=True,
                      o_fw=o_fw, head_norm=odd_head_norm[i])
            h = _mm_res(cfg, [og], ow_out, (i,), h, 1.0)
        h = ffn(h, layer, 1, 2)
    return h.reshape(bsz, cfg.lp, d)[:, cfg.n_pad + cfg.n_meta:]


def kernel(x, meta_tokens, pre_norm, ffn_w_gate, ffn_w_up, ffn_w_down, even_w_in, even_q_norm,
           even_k_norm, even_w_out, odd_w_in, odd_gate_a, odd_gate_b, odd_gate_bias,
           odd_head_norm, odd_w_out):
    bsz, seq, d = x.shape
    cfg = Cfg(batch=bsz, seq=seq, d_model=d, d_ff=ffn_w_gate.shape[-1], depth=pre_norm.shape[0],
              n_meta=N_META, grid_w=GRID_W, attn_heads=ATTN_HEADS, attn_kv_heads=ATTN_KV_HEADS,
              head_dim=HEAD_DIM, q_block=Q_BLOCK, four_groups=FOURIER_GROUPS,
              four_dim=FOURIER_GROUP_DIM, gla_heads=GLA_HEADS, gla_rank=odd_gate_a.shape[-1],
              tm_norm=320, tm=640, tn=512, attn_tk=128, attn_unroll=16, dft_tm=1664, dft_tn=1024, dft_tk=640,
              gla_chunk=128, gla_sub=4, gla_heads_per_step=4)
    return _forward(cfg, x, meta_tokens, pre_norm, ffn_w_gate, ffn_w_up, ffn_w_down, even_w_in,
                    even_q_norm, even_k_norm, even_w_out, odd_w_in, odd_gate_a, odd_gate_b,
                    odd_gate_bias, odd_head_norm, odd_w_out)
```

```python
import functools
import math
from typing import NamedTuple

import jax
import jax.numpy as jnp
from jax import lax
from jax.experimental import pallas as pl
from jax.experimental.pallas import tpu as pltpu

F32 = jnp.float32
BF16 = jnp.bfloat16

N_META = 16
GRID_W = 64
EPS = 1e-6
ATTN_HEADS = 16
ATTN_KV_HEADS = 4
HEAD_DIM = 128
Q_BLOCK = 128
ROPE_THETA = 10000.0
FOURIER_GROUPS = 16
FOURIER_GROUP_DIM = 128
GLA_HEADS = 8
GLA_GATE_TAU = 16.0

V7X_VMEM_BYTES = 64 * 1024 * 1024
VMEM_LIMIT_BYTES = V7X_VMEM_BYTES - 8 * 1024 * 1024
LANES = 128
SUBLANES = 8
NEG_BIG = -1e30


class Cfg(NamedTuple):
    batch: int
    seq: int
    d_model: int
    d_ff: int
    depth: int
    n_meta: int
    grid_w: int
    attn_heads: int
    attn_kv_heads: int
    head_dim: int
    q_block: int
    four_groups: int
    four_dim: int
    gla_heads: int
    gla_rank: int
    tm_res: int
    tm: int
    tn: int
    attn_tk: int
    attn_unroll: int
    dft_tm: int
    dft_tn: int
    dft_tk: int
    gla_chunk: int
    gla_sub: int
    gla_heads_per_step: int

    @property
    def length(self):
        return self.n_meta + self.seq

    @property
    def n_pad(self):
        return (-self.length) % self.q_block

    @property
    def lp(self):
        return self.length + self.n_pad

    @property
    def rows(self):
        return self.batch * self.lp

    @property
    def q_w(self):
        return self.attn_heads * self.head_dim

    @property
    def kv_w(self):
        return self.attn_kv_heads * self.head_dim

    @property
    def four_w(self):
        return self.four_groups * self.four_dim

    @property
    def gla_dk(self):
        return self.d_model // (2 * self.gla_heads)

    @property
    def gla_dv(self):
        return self.d_model // self.gla_heads


def _params(*sem):
    return pltpu.CompilerParams(dimension_semantics=sem, vmem_limit_bytes=VMEM_LIMIT_BYTES)


def _dot(a, b):
    return jnp.dot(a, b, preferred_element_type=F32)


def _dot_nt(a, b):
    return lax.dot_general(a, b, (((1,), (1,)), ((), ())), preferred_element_type=F32)


def _dot_tn(a, b):
    return lax.dot_general(a, b, (((0,), (0,)), ((), ())), preferred_element_type=F32)


NORM_ROWS = 128


def _normalize_rows(h_ref, g_ref, xn_ref, eps):
    @pl.when(pl.program_id(1) == 0)
    def _():
        gain = g_ref[...]

        def body(r, carry):
            rows = pl.ds(pl.multiple_of(r * NORM_ROWS, NORM_ROWS), NORM_ROWS)
            x = h_ref[rows, :]
            ms = jnp.mean(x * x, axis=-1, keepdims=True)
            xn_ref[rows, :] = (x * lax.rsqrt(ms + eps) * gain).astype(xn_ref.dtype)
            return carry

        lax.fori_loop(0, h_ref.shape[0] // NORM_ROWS, body, 0)


def _norm_specs(tm, d):
    return [pl.BlockSpec((tm, d), lambda i, j: (i, 0)), pl.BlockSpec((1, d), lambda i, j: (0, 0))]


def _ffn_up_kernel(h_ref, g_ref, wg_ref, wu_ref, o_ref, xn_ref, *, eps):
    _normalize_rows(h_ref, g_ref, xn_ref, eps)
    x = xn_ref[...]
    g = _dot(x, wg_ref[...])
    u = _dot(x, wu_ref[...])
    o_ref[...] = (g * jax.nn.sigmoid(g) * u).astype(o_ref.dtype)


def _ffn_up(cfg, h, gain, wg, wu, layer, slot):
    m, d = h.shape
    f = wg.shape[-1]
    tm, tn = cfg.tm, cfg.tn
    assert tm % NORM_ROWS == 0
    wspec = pl.BlockSpec((None, None, d, tn), lambda i, j: (layer, slot, 0, j))
    return pl.pallas_call(
        functools.partial(_ffn_up_kernel, eps=EPS),
        out_shape=jax.ShapeDtypeStruct((m, f), BF16),
        grid=(m // tm, f // tn),
        in_specs=_norm_specs(tm, d) + [wspec, wspec],
        out_specs=pl.BlockSpec((tm, tn), lambda i, j: (i, j)),
        scratch_shapes=[pltpu.VMEM((tm, d), BF16)],
        compiler_params=_params("parallel", "arbitrary"),
        name="ffn_up",
    )(h, gain.reshape(1, d), wg, wu)


def _mm_res_kernel(*refs, n_lhs, scale):
    xs, ws = refs[:n_lhs], refs[n_lhs:2 * n_lhs]
    h_ref, o_ref = refs[2 * n_lhs], refs[2 * n_lhs + 1]
    acc = _dot(xs[0][...], ws[0][...])
    for x_ref, w_ref in zip(xs[1:], ws[1:]):
        acc = acc + _dot(x_ref[...], w_ref[...])
    o_ref[...] = h_ref[...] + scale * acc


def _mm_res(cfg, xs, w, w_lead, h, scale):
    m, d = h.shape
    kx = xs[0].shape[1]
    n_lhs = len(xs)
    tm, tn = cfg.tm_res, cfg.tn
    n_lead = len(w_lead)
    w_specs = [
        pl.BlockSpec((None,) * n_lead + (kx, tn), lambda i, j, p=p: tuple(w_lead) + (p, j))
        for p in range(n_lhs)
    ]
    return pl.pallas_call(
        functools.partial(_mm_res_kernel, n_lhs=n_lhs, scale=scale),
        out_shape=jax.ShapeDtypeStruct((m, d), F32),
        grid=(m // tm, d // tn),
        in_specs=[pl.BlockSpec((tm, kx), lambda i, j: (i, 0))] * n_lhs + w_specs
        + [pl.BlockSpec((tm, tn), lambda i, j: (i, j))],
        out_specs=pl.BlockSpec((tm, tn), lambda i, j: (i, j)),
        input_output_aliases={2 * n_lhs: 0},
        compiler_params=_params("parallel", "arbitrary"),
        name="mm_residual",
    )(*xs, *([w] * n_lhs), h)


def _even_inproj_kernel(h_ref, g_ref, w_ref, cos_ref, sin_ref, qg_ref, kg_ref, o_ref, qt_ref,
                        vt_ref, xn_ref, *, n_q_tiles, n_k_tiles, hd, eps, q_scale):
    _normalize_rows(h_ref, g_ref, xn_ref, eps)
    j = pl.program_id(1)
    acc = _dot(xn_ref[...], w_ref[...])
    heads_per_tile = acc.shape[1] // hd

    def normed_rotated(gain):
        c = cos_ref[...]
        s = sin_ref[...]
        lane = lax.broadcasted_iota(jnp.int32, c.shape, 1)
        first_half = (lane % (hd // 2)) < (hd // 4)
        for hh in range(heads_per_tile):
            xh = acc[:, hh * hd:(hh + 1) * hd]
            ms = jnp.mean(xh * xh, axis=-1, keepdims=True)
            y = xh * lax.rsqrt(ms + eps) * gain
            partner = jnp.where(first_half,
                                pltpu.roll(y, hd - hd // 4, axis=1),
                                pltpu.roll(y, hd // 4, axis=1))
            yield hh, y * c + partner * s

    @pl.when(j < n_q_tiles)
    def _():
        for hh, y in normed_rotated(qg_ref[...] * q_scale):
            o_ref[:, hh * hd:(hh + 1) * hd] = y.astype(o_ref.dtype)
            qt_ref[hh * hd:(hh + 1) * hd, :] = y.T.astype(qt_ref.dtype)

    @pl.when((j >= n_q_tiles) & (j < n_q_tiles + n_k_tiles))
    def _():
        for hh, y in normed_rotated(kg_ref[...]):
            o_ref[:, hh * hd:(hh + 1) * hd] = y.astype(o_ref.dtype)

    @pl.when(j >= n_q_tiles + n_k_tiles)
    def _():
        o_ref[...] = acc.astype(o_ref.dtype)

    @pl.when((j >= n_q_tiles + n_k_tiles) & (j < n_q_tiles + 2 * n_k_tiles))
    def _():
        vt_ref[...] = acc.T.astype(vt_ref.dtype)


def _even_inproj(cfg, h, gain, w, layer_i, cos_t, sin_t, q_gain, k_gain):
    m, d = h.shape
    n = w.shape[-1]
    tm, tn, hd = cfg.tm, cfg.tn, cfg.head_dim
    assert cfg.q_w % tn == 0 and cfg.kv_w % tn == 0 and tn % hd == 0 and tm % NORM_ROWS == 0
    n_q, n_k = cfg.q_w // tn, cfg.kv_w // tn
    kern = functools.partial(
        _even_inproj_kernel, n_q_tiles=n_q, n_k_tiles=n_k, hd=hd,
        eps=EPS, q_scale=hd ** -0.5 * math.log2(math.e))
    return pl.pallas_call(
        kern,
        out_shape=(jax.ShapeDtypeStruct((m, n), BF16),
                   jax.ShapeDtypeStruct((cfg.q_w, m), BF16),
                   jax.ShapeDtypeStruct((cfg.kv_w, m), BF16)),
        grid=(m // tm, n // tn),
        in_specs=_norm_specs(tm, d) + [
            pl.BlockSpec((None, d, tn), lambda i, j: (layer_i, 0, j)),
            pl.BlockSpec((tm, hd), lambda i, j: (i, 0)),
            pl.BlockSpec((tm, hd), lambda i, j: (i, 0)),
            pl.BlockSpec((1, hd), lambda i, j: (0, 0)),
            pl.BlockSpec((1, hd), lambda i, j: (0, 0))],
        out_specs=(
            pl.BlockSpec((tm, tn), lambda i, j: (i, j)),
            pl.BlockSpec((tn, tm), lambda i, j: (jnp.minimum(j, n_q - 1), i)),
            pl.BlockSpec((tn, tm), lambda i, j: (jnp.clip(j - n_q - n_k, 0, n_k - 1), i))),
        scratch_shapes=[pltpu.VMEM((tm, d), BF16)],
        compiler_params=_params("parallel", "arbitrary"),
        name="even_inproj",
    )(h, gain.reshape(1, d), w, cos_t, sin_t, q_gain.reshape(1, hd), k_gain.reshape(1, hd))


def _attn_kernel(qt_ref, k_ref, vt_ref, o_ref, acc_ref, *, groups, hpc, hd, sub, n_sub, n_pad,
                 unroll):
    tq = qt_ref.shape[1]
    n_chain = groups // hpc
    cw = hpc * tq
    q_t = [jnp.concatenate([qt_ref[(c * hpc + g) * hd:(c * hpc + g + 1) * hd, :]
                            for g in range(hpc)], axis=1) for c in range(n_chain)]

    def scores(j):
        k_blk = k_ref[pl.ds(pl.multiple_of(j * sub, sub), sub), :]
        return tuple(_dot(k_blk, q_t[c]) for c in range(n_chain))

    def consume(j, st, ml, masked):
        v_blk = vt_ref[:, pl.ds(pl.multiple_of(j * sub, sub), sub)]
        out = []
        for c in range(n_chain):
            m_prev, l_prev = ml[c]
            s = st[c]
            if masked:
                row = lax.broadcasted_iota(jnp.int32, s.shape, 0)
                s = jnp.where(row < n_pad, NEG_BIG, s)
            m_new = jnp.maximum(m_prev, jnp.max(s, axis=0, keepdims=True))
            alpha = jnp.exp2(m_prev - m_new)
            p = jnp.exp2(s - m_new)
            l_new = alpha * l_prev + jnp.sum(p, axis=0, keepdims=True)
            acc_ref[c] = alpha * acc_ref[c] + _dot(v_blk, p.astype(v_blk.dtype))
            out.append((m_new, l_new))
        return tuple(out)

    acc_ref[...] = jnp.zeros_like(acc_ref)
    ml = tuple((jnp.full((1, cw), NEG_BIG, F32), jnp.zeros((1, cw), F32))
               for _ in range(n_chain))
    st = scores(0)
    st_next = scores(1)
    ml = consume(0, st, ml, True)

    def body(it, carry):
        st, ml = carry
        for u in range(unroll):
            j = 1 + it * unroll + u
            st_next = scores(jnp.minimum(j + 1, n_sub - 1))
            ml = consume(j, st, ml, False)
            st = st_next
        return st, ml

    _, ml = lax.fori_loop(0, (n_sub - 1) // unroll, body, (st_next, ml))
    for c in range(n_chain):
        o_t = acc_ref[c] / ml[c][1]
        for g in range(hpc):
            h0 = (c * hpc + g) * hd
            o_ref[:, h0:h0 + hd] = o_t[:, g * tq:(g + 1) * tq].T.astype(o_ref.dtype)


def _attention(cfg, u, qt, vt):
    m = u.shape[0]
    hd, kv, tq, lp = cfg.head_dim, cfg.attn_kv_heads, cfg.q_block, cfg.lp
    groups = cfg.attn_heads // kv
    hpc = min(groups, 2)
    n_q = lp // tq
    sub = cfg.attn_tk
    n_sub = lp // sub
    assert cfg.n_pad <= sub and lp % sub == 0 and (n_sub - 1) % cfg.attn_unroll == 0
    kern = functools.partial(_attn_kernel, groups=groups, hpc=hpc, hd=hd, sub=sub,
                             n_sub=n_sub, n_pad=cfg.n_pad, unroll=cfg.attn_unroll)
    k_col0 = cfg.q_w // hd
    return pl.pallas_call(
        kern,
        out_shape=jax.ShapeDtypeStruct((m, cfg.q_w), BF16),
        grid=(cfg.batch, kv, n_q),
        in_specs=[pl.BlockSpec((groups * hd, tq), lambda b, h, i: (h, b * n_q + i)),
                  pl.BlockSpec((lp, hd), lambda b, h, i: (b, k_col0 + h)),
                  pl.BlockSpec((hd, lp), lambda b, h, i: (h, b))],
        out_specs=pl.BlockSpec((tq, groups * hd), lambda b, h, i: (b * n_q + i, h)),
        scratch_shapes=[pltpu.VMEM((groups // hpc, hd, hpc * tq), F32)],
        compiler_params=_params("parallel", "parallel", "arbitrary"),
        name="gqa_attention",
    )(qt, u, vt)


def _chan_dft_kernel(f_ref, cs_ref, a_ref, b_ref, *, gd):
    cs = cs_ref[...]
    for g in range(f_ref.shape[1] // gd):
        ab = _dot(f_ref[:, g * gd:(g + 1) * gd], cs)
        a_ref[:, g * gd:(g + 1) * gd] = ab[:, :gd].astype(a_ref.dtype)
        b_ref[:, g * gd:(g + 1) * gd] = ab[:, gd:].astype(b_ref.dtype)


def _chan_dft(cfg, u, cs):
    m = u.shape[0]
    gd, fw = cfg.four_dim, cfg.four_w
    f0 = cfg.q_w + 2 * cfg.kv_w
    fb = math.gcd(math.gcd(f0, fw), 1024)
    tm = cfg.tm
    out = jax.ShapeDtypeStruct((m, fw), BF16)
    return pl.pallas_call(
        functools.partial(_chan_dft_kernel, gd=gd),
        out_shape=(out, out),
        grid=(m // tm, fw // fb),
        in_specs=[pl.BlockSpec((tm, fb), lambda i, j: (i, f0 // fb + j)),
                  pl.BlockSpec((gd, 2 * gd), lambda i, j: (0, 0))],
        out_specs=(pl.BlockSpec((tm, fb), lambda i, j: (i, j)),
                   pl.BlockSpec((tm, fb), lambda i, j: (i, j))),
        compiler_params=_params("parallel", "parallel"),
        name="fourier_channel_dft",
    )(u, cs)


def _seq_dft_kernel(wc_ref, ws_ref, a_ref, b_ref, o_ref, acc_ref):
    k = pl.program_id(3)

    @pl.when(k == 0)
    def _():
        acc_ref[...] = jnp.zeros_like(acc_ref)

    acc_ref[...] += _dot(wc_ref[...], a_ref[...]) - _dot(ws_ref[...], b_ref[...])

    @pl.when(k == pl.num_programs(3) - 1)
    def _():
        o_ref[...] = acc_ref[...].astype(o_ref.dtype)


def _seq_dft(cfg, wc, ws, a, b):
    m, fw = a.shape
    lp = cfg.lp
    tm, tn, tk = cfg.dft_tm, cfg.dft_tn, cfg.dft_tk
    n_m, n_k = lp // tm, lp // tk
    return pl.pallas_call(
        _seq_dft_kernel,
        out_shape=jax.ShapeDtypeStruct((m, fw), BF16),
        grid=(cfg.batch, fw // tn, n_m, n_k),
        in_specs=[pl.BlockSpec((tm, tk), lambda bb, n, i, k: (i, k)),
                  pl.BlockSpec((tm, tk), lambda bb, n, i, k: (i, k)),
                  pl.BlockSpec((tk, tn), lambda bb, n, i, k: (bb * n_k + k, n)),
                  pl.BlockSpec((tk, tn), lambda bb, n, i, k: (bb * n_k + k, n))],
        out_specs=pl.BlockSpec((tm, tn), lambda bb, n, i, k: (bb * n_m + i, n)),
        scratch_shapes=[pltpu.VMEM((tm, tn), F32)],
        compiler_params=_params("parallel", "parallel", "parallel", "arbitrary"),
        name="fourier_sequence_dft",
    )(wc, ws, a, b)


def _odd_inproj_kernel(h_ref, g_ref, w_ref, ga_ref, o_ref, low_ref, xn_ref, *, n_q_tiles,
                       q_scale, eps):
    _normalize_rows(h_ref, g_ref, xn_ref, eps)
    x = xn_ref[...]

    @pl.when(pl.program_id(1) == 0)
    def _():
        low_ref[...] = _dot(x, ga_ref[...])

    acc = _dot(x, w_ref[...])
    sc = jnp.where(pl.program_id(1) < n_q_tiles, q_scale, 1.0).astype(F32)
    o_ref[...] = (acc * sc).astype(o_ref.dtype)


def _odd_inproj(cfg, h, gain, w, layer_i, ga):
    m, d = h.shape
    n = w.shape[-1]
    r2 = ga.shape[-1]
    tm, tn = cfg.tm, cfg.tn
    assert tm % NORM_ROWS == 0
    kern = functools.partial(_odd_inproj_kernel, n_q_tiles=cfg.gla_heads * cfg.gla_dk // tn,
                             q_scale=cfg.gla_dk ** -0.5, eps=EPS)
    return pl.pallas_call(
        kern,
        out_shape=(jax.ShapeDtypeStruct((m, n), BF16), jax.ShapeDtypeStruct((m, r2), F32)),
        grid=(m // tm, n // tn),
        in_specs=_norm_specs(tm, d) + [
            pl.BlockSpec((None, d, tn), lambda i, j: (layer_i, 0, j)),
            pl.BlockSpec((None, d, r2), lambda i, j: (layer_i, 0, 0))],
        out_specs=(pl.BlockSpec((tm, tn), lambda i, j: (i, j)),
                   pl.BlockSpec((tm, r2), lambda i, j: (i, 0))),
        scratch_shapes=[pltpu.VMEM((tm, d), BF16)],
        compiler_params=_params("parallel", "arbitrary"),
        name="odd_inproj",
    )(h, gain.reshape(1, d), w, ga)


def _gla_kernel(*refs, rev, chunk, sub, halo, hpb, dk, dv, n_chunks, n_pad, tau, eps):
    if rev:
        (q_ref, k_ref, v_ref, low_ref, gb_ref, bias_ref, tri_ref, lmask_ref,
         ofw_ref, r_ref, hn_ref, o_ref, s_ref, cump_ref, kfp_ref) = refs
    else:
        (q_ref, k_ref, v_ref, low_ref, gb_ref, bias_ref, tri_ref, lmask_ref,
         o_ref, s_ref, cump_ref, kfp_ref) = refs
    c = pl.program_id(2)
    pos_chunk = (n_chunks - 1 - c) if rev else c

    @pl.when(c == 0)
    def _():
        s_ref[...] = jnp.zeros_like(s_ref)
        cump_ref[...] = jnp.zeros_like(cump_ref)
        kfp_ref[...] = jnp.zeros_like(kfp_ref)

    row = pos_chunk * chunk + lax.broadcasted_iota(jnp.int32, (chunk, 1), 0)
    valid = row >= n_pad
    col_minus_row = (lax.broadcasted_iota(jnp.int32, (chunk, chunk), 1)
                     - lax.broadcasted_iota(jnp.int32, (chunk, chunk), 0))
    row_in_sub = lax.broadcasted_iota(jnp.int32, (chunk, 1), 0) % sub
    low = low_ref[...]
    tri = tri_ref[...]

    def head(hx):
        ks = slice(hx * dk, (hx + 1) * dk)
        vs = slice(hx * dv, (hx + 1) * dv)
        z = jnp.dot(low, gb_ref[:, ks], preferred_element_type=F32,
                    precision=lax.Precision.HIGHEST) + bias_ref[:, ks]
        log_a = (jnp.minimum(z, 0.0) - jnp.log(1.0 + jnp.exp(-jnp.abs(z)))) * (1.0 / tau)
        log_a = jnp.where(valid, log_a, 0.0)
        kf = jnp.where(valid, k_ref[:, ks].astype(F32), 0.0)
        qf = q_ref[:, ks].astype(F32)
        v = v_ref[:, vs]

        g_hi = log_a.astype(BF16)
        g_lo = (log_a - g_hi.astype(F32)).astype(BF16)
        cum = _dot(tri, g_hi) + _dot(tri, g_lo)
        cum_end = cum[0:1] if rev else cum[chunk - 1:chunk]
        yield

        s_old = s_ref[hx]
        inter = _dot_nt((qf * jnp.exp(cum)).astype(BF16), s_old.astype(BF16))
        yield

        a_acc = jnp.zeros((chunk, chunk), F32)
        s_blk, level = chunk // 2, 0
        while s_blk >= sub:
            grp = 2 * s_blk
            ref_idx = s_blk if rev else s_blk - 1
            ref = jnp.broadcast_to(
                cum.reshape(chunk // grp, grp, dk)[:, ref_idx:ref_idx + 1, :],
                (chunk // grp, grp, dk)).reshape(chunk, dk)
            e = jnp.exp(-jnp.abs(cum - ref))
            a_lvl = _dot_nt((qf * e).astype(BF16), (kf * e).astype(BF16))
            a_acc = a_acc + a_lvl * lmask_ref[level]
            s_blk //= 2
            level += 1
            yield

        cump_ref[hx, halo:halo + chunk, :] = cum
        kfp_ref[hx, halo:halo + chunk, :] = kf
        a_acc = a_acc + jnp.where(col_minus_row == 0,
                                  jnp.sum(qf * kf, axis=-1, keepdims=True), 0.0)
        for delta in range(1, sub):
            off = halo + delta if rev else halo - delta
            k_sh = kfp_ref[hx, off:off + chunk, :]
            cum_sh = cump_ref[hx, off:off + chunk, :]
            col = jnp.sum(qf * k_sh * jnp.exp(cum - cum_sh), axis=-1, keepdims=True)
            same_sub = (row_in_sub + delta < sub) if rev else (row_in_sub >= delta)
            col = jnp.where(same_sub, col, 0.0)
            a_acc = a_acc + jnp.where(col_minus_row == (delta if rev else -delta), col, 0.0)
            yield

        o = inter + _dot(a_acc.astype(BF16), v)

        k_out = (kf * jnp.exp(cum_end - cum)).astype(BF16)
        s_ref[hx] = s_old * jnp.exp(cum_end) + _dot_tn(v, k_out)
        yield

        if rev:
            tot = ofw_ref[:, vs] + o
            ms = jnp.mean(tot * tot, axis=-1, keepdims=True)
            y = tot * lax.rsqrt(ms + eps) * hn_ref[...]
            r = r_ref[:, vs].astype(F32)
            o_ref[:, vs] = (y * r * jax.nn.sigmoid(r)).astype(o_ref.dtype)
        else:
            o_ref[:, vs] = o

    chains = [head(hx) for hx in range(hpb)]
    while chains:
        for g in list(chains):
            if next(g, "done") == "done":
                chains.remove(g)


def _gla_constants(cfg, rev):
    c, sub = cfg.gla_chunk, cfg.gla_sub
    i = jnp.arange(c)[:, None]
    j = jnp.arange(c)[None, :]
    tri = ((j >= i) if rev else (j <= i)).astype(BF16)
    masks = []
    s = c // 2
    while s >= sub:
        i_hi, j_hi = (i // s) % 2 == 1, (j // s) % 2 == 1
        roles = (~i_hi & j_hi) if rev else (i_hi & ~j_hi)
        masks.append(((i // (2 * s) == j // (2 * s)) & roles).astype(F32))
        s //= 2
    return tri, jnp.stack(masks)


def _gla(cfg, u, low, gate_b, gate_bias, rev, o_fw=None, head_norm=None):
    m = u.shape[0]
    hh, dk, dv, c = cfg.gla_heads, cfg.gla_dk, cfg.gla_dv, cfg.gla_chunk
    rank = cfg.gla_rank
    n_chunks = cfg.lp // c
    tri, lmask = _gla_constants(cfg, rev)
    n_lvl = lmask.shape[0]
    hpb = cfg.gla_heads_per_step
    kw, vw = hpb * dk, hpb * dv
    assert hh % hpb == 0 and (2 * hh * dk) % vw == 0
    k_col0 = hh // hpb
    v_col0 = 2 * hh * dk // vw
    r_col0 = v_col0 + hh // hpb

    def rowblk(b, h, ci):
        return b * n_chunks + ((n_chunks - 1 - ci) if rev else ci)

    in_specs = [
        pl.BlockSpec((c, kw), lambda b, h, ci: (rowblk(b, h, ci), h)),
        pl.BlockSpec((c, kw), lambda b, h, ci: (rowblk(b, h, ci), k_col0 + h)),
        pl.BlockSpec((c, vw), lambda b, h, ci: (rowblk(b, h, ci), v_col0 + h)),
        pl.BlockSpec((c, rank), lambda b, h, ci: (rowblk(b, h, ci), 0)),
        pl.BlockSpec((rank, kw), lambda b, h, ci: (0, h)),
        pl.BlockSpec((1, kw), lambda b, h, ci: (0, h)),
        pl.BlockSpec((c, c), lambda b, h, ci: (0, 0)),
        pl.BlockSpec((n_lvl, c, c), lambda b, h, ci: (0, 0, 0)),
    ]
    args = [u, u, u, low, gate_b, gate_bias.reshape(1, -1), tri, lmask]
    if rev:
        in_specs += [
            pl.BlockSpec((c, vw), lambda b, h, ci: (rowblk(b, h, ci), h)),
            pl.BlockSpec((c, vw), lambda b, h, ci: (rowblk(b, h, ci), r_col0 + h)),
            pl.BlockSpec((1, dv), lambda b, h, ci: (0, 0)),
        ]
        args += [o_fw, u, head_norm.reshape(1, dv)]
    halo = -(-cfg.gla_sub // SUBLANES) * SUBLANES
    kern = functools.partial(_gla_kernel, rev=rev, chunk=c, sub=cfg.gla_sub, halo=halo,
                             hpb=hpb, dk=dk, dv=dv, n_chunks=n_chunks, n_pad=cfg.n_pad,
                             tau=GLA_GATE_TAU, eps=EPS)
    return pl.pallas_call(
        kern,
        out_shape=jax.ShapeDtypeStruct((m, hh * dv), BF16 if rev else F32),
        grid=(cfg.batch, hh // hpb, n_chunks),
        in_specs=in_specs,
        out_specs=pl.BlockSpec((c, vw), lambda b, h, ci: (rowblk(b, h, ci), h)),
        scratch_shapes=[pltpu.VMEM((hpb, dv, dk), F32),
                        pltpu.VMEM((hpb, c + 2 * halo, dk), F32),
                        pltpu.VMEM((hpb, c + 2 * halo, dk), F32)],
        compiler_params=_params("parallel", "parallel", "arbitrary"),
        name="gla_reverse" if rev else "gla_forward",
    )(*args)


def _rope_tables(cfg):
    hd = cfg.head_dim
    n_freq = hd // 4
    t = jnp.arange(cfg.seq)
    rows = (t // cfg.grid_w).astype(F32)
    cols = (t % cfg.grid_w).astype(F32)
    inv_freq = jnp.power(ROPE_THETA, -jnp.arange(n_freq, dtype=F32) / n_freq)
    ang = jnp.concatenate([jnp.tile(rows[:, None] * inv_freq, (1, 2)),
                           jnp.tile(cols[:, None] * inv_freq, (1, 2))], axis=1)
    ang = jnp.concatenate([jnp.zeros((cfg.n_pad + cfg.n_meta, hd), F32), ang], axis=0)
    sign = jnp.where((jnp.arange(hd) % (hd // 2)) < n_freq, -1.0, 1.0).astype(F32)
    cos_t = jnp.tile(jnp.cos(ang), (cfg.batch, 1))
    sin_t = jnp.tile(jnp.sin(ang) * sign, (cfg.batch, 1))
    return cos_t, sin_t


def _dft_tables(cfg):
    ll, gd, lp, n_pad = cfg.length, cfg.four_dim, cfg.lp, cfg.n_pad
    scale = 1.0 / math.sqrt(ll * gd)
    n_hi = -(-ll // LANES)
    t = jnp.arange(ll, dtype=jnp.int32)[None, :]
    k_lo = jnp.arange(LANES, dtype=jnp.int32)[:, None]
    k_hi = jnp.arange(n_hi, dtype=jnp.int32)[:, None] * LANES
    ang_lo = (2.0 * math.pi / ll) * ((k_lo * t) % ll).astype(F32)
    ang_hi = (2.0 * math.pi / ll) * ((k_hi * t) % ll).astype(F32)
    c_lo, s_lo = jnp.cos(ang_lo)[None], jnp.sin(ang_lo)[None]
    c_hi, s_hi = (jnp.cos(ang_hi) * scale)[:, None], (jnp.sin(ang_hi) * scale)[:, None]
    wc = (c_hi * c_lo - s_hi * s_lo).reshape(n_hi * LANES, ll)[:ll]
    ws = (s_hi * c_lo + c_hi * s_lo).reshape(n_hi * LANES, ll)[:ll]
    pad = ((n_pad, lp - n_pad - ll), (n_pad, lp - n_pad - ll))
    wc = jnp.pad(wc.astype(BF16), pad)
    ws = jnp.pad(ws.astype(BF16), pad)
    cc = jnp.arange(gd, dtype=jnp.int32)
    ang_c = (2.0 * math.pi / gd) * ((cc[:, None] * cc[None, :]) % gd).astype(F32)
    cs = jnp.concatenate([jnp.cos(ang_c), jnp.sin(ang_c)], axis=1).astype(BF16)
    return wc, ws, cs


def _forward(cfg, x, meta_tokens, pre_norm, ffn_w_gate, ffn_w_up, ffn_w_down, even_w_in,
             even_q_norm, even_k_norm, even_w_out, odd_w_in, odd_gate_a, odd_gate_b,
             odd_gate_bias, odd_head_norm, odd_w_out):
    bsz, d = cfg.batch, cfg.d_model
    meta = jnp.broadcast_to(meta_tokens.astype(F32)[None], (bsz, cfg.n_meta, d))
    h = jnp.concatenate([jnp.zeros((bsz, cfg.n_pad, d), F32), meta, x.astype(F32)], axis=1)
    h = h.reshape(cfg.rows, d)

    wg, wu, wd = ffn_w_gate.astype(BF16), ffn_w_up.astype(BF16), ffn_w_down.astype(BF16)
    ew_in, ew_out = even_w_in.astype(BF16), even_w_out.astype(BF16)
    ow_in, ow_out = odd_w_in.astype(BF16), odd_w_out.astype(BF16)
    ga = jnp.concatenate([odd_gate_a[:, 0], odd_gate_a[:, 1]], axis=-1).astype(BF16)

    cos_t, sin_t = _rope_tables(cfg)
    wc, ws, cs = _dft_tables(cfg)

    def ffn(h, layer, slot, norm_slot):
        a = _ffn_up(cfg, h, pre_norm[layer, norm_slot], wg, wu, layer, slot)
        return _mm_res(cfg, [a], wd, (layer, slot), h, 0.5)

    for layer in range(cfg.depth):
        i = layer // 2
        h = ffn(h, layer, 0, 0)
        mix_gain = pre_norm[layer, 1]
        if layer % 2 == 0:
            u, qt, vt = _even_inproj(cfg, h, mix_gain, ew_in, i, cos_t, sin_t, even_q_norm[i],
                                     even_k_norm[i])
            attn = _attention(cfg, u, qt, vt)
            fa, fb = _chan_dft(cfg, u, cs)
            four = _seq_dft(cfg, wc, ws, fa, fb)
            h = _mm_res(cfg, [attn, four], ew_out, (i,), h, 1.0)
        else:
            u, low = _odd_inproj(cfg, h, mix_gain, ow_in, i, ga)
            rank = cfg.gla_rank
            o_fw = _gla(cfg, u, low[:, :rank], odd_gate_b[i, 0], odd_gate_bias[i, 0], rev=False)
            og = _gla(cfg, u, low[:, rank:], odd_gate_b[i, 1], odd_gate_bias[i, 1], rev=True,
                      o_fw=o_fw, head_norm=odd_head_norm[i])
            h = _mm_res(cfg, [og], ow_out, (i,), h, 1.0)
        h = ffn(h, layer, 1, 2)
    return h.reshape(bsz, cfg.lp, d)[:, cfg.n_pad + cfg.n_meta:]


def kernel(x, meta_tokens, pre_norm, ffn_w_gate, ffn_w_up, ffn_w_down, even_w_in, even_q_norm,
           even_k_norm, even_w_out, odd_w_in, odd_gate_a, odd_gate_b, odd_gate_bias,
           odd_head_norm, odd_w_out):
    bsz, seq, d = x.shape
    cfg = Cfg(batch=bsz, seq=seq, d_model=d, d_ff=ffn_w_gate.shape[-1], depth=pre_norm.shape[0],
              n_meta=N_META, grid_w=GRID_W, attn_heads=ATTN_HEADS, attn_kv_heads=ATTN_KV_HEADS,
              head_dim=HEAD_DIM, q_block=Q_BLOCK, four_groups=FOURIER_GROUPS,
              four_dim=FOURIER_GROUP_DIM, gla_heads=GLA_HEADS, gla_rank=odd_gate_a.shape[-1],
              tm_res=1280, tm=640, tn=512, attn_tk=128, attn_unroll=16, dft_tm=1664, dft_tn=1024, dft_tk=640,
              gla_chunk=128, gla_sub=4, gla_heads_per_step=4)
    return _forward(cfg, x, meta_tokens, pre_norm, ffn_w_gate, ffn_w_up, ffn_w_down, even_w_in,
                    even_q_norm, even_k_norm, even_w_out, odd_w_in, odd_gate_a, odd_gate_b,
                    odd_gate_bias, odd_head_norm, odd_w_out)
```

```python
import functools
import math
from typing import NamedTuple

import jax
import jax.numpy as jnp
from jax import lax
from jax.experimental import pallas as pl
from jax.experimental.pallas import tpu as pltpu

F32 = jnp.float32
BF16 = jnp.bfloat16

N_META = 16
GRID_W = 64
EPS = 1e-6
ATTN_HEADS = 16
ATTN_KV_HEADS = 4
HEAD_DIM = 128
Q_BLOCK = 128
ROPE_THETA = 10000.0
FOURIER_GROUPS = 16
FOURIER_GROUP_DIM = 128
GLA_HEADS = 8
GLA_GATE_TAU = 16.0

V7X_VMEM_BYTES = 64 * 1024 * 1024
VMEM_LIMIT_BYTES = V7X_VMEM_BYTES - 8 * 1024 * 1024
LANES = 128
SUBLANES = 8
NEG_BIG = -1e30


class Cfg(NamedTuple):
    batch: int
    seq: int
    d_model: int
    d_ff: int
    depth: int
    n_meta: int
    grid_w: int
    attn_heads: int
    attn_kv_heads: int
    head_dim: int
    q_block: int
    four_groups: int
    four_dim: int
    gla_heads: int
    gla_rank: int
    tm_res: int
    tm: int
    tn: int
    attn_tk: int
    attn_unroll: int
    dft_tm: int
    dft_tn: int
    dft_tk: int
    gla_chunk: int
    gla_sub: int
    gla_heads_per_step: int

    @property
    def length(self):
        return self.n_meta + self.seq

    @property
    def n_pad(self):
        return (-self.length) % self.q_block

    @property
    def lp(self):
        return self.length + self.n_pad

    @property
    def rows(self):
        return self.batch * self.lp

    @property
    def q_w(self):
        return self.attn_heads * self.head_dim

    @property
    def kv_w(self):
        return self.attn_kv_heads * self.head_dim

    @property
    def four_w(self):
        return self.four_groups * self.four_dim

    @property
    def gla_dk(self):
        return self.d_model // (2 * self.gla_heads)

    @property
    def gla_dv(self):
        return self.d_model // self.gla_heads


def _params(*sem):
    return pltpu.CompilerParams(dimension_semantics=sem, vmem_limit_bytes=VMEM_LIMIT_BYTES)


def _dot(a, b):
    return jnp.dot(a, b, preferred_element_type=F32)


def _dot_nt(a, b):
    return lax.dot_general(a, b, (((1,), (1,)), ((), ())), preferred_element_type=F32)


def _dot_tn(a, b):
    return lax.dot_general(a, b, (((0,), (0,)), ((), ())), preferred_element_type=F32)


NORM_ROWS = 128


def _normalize_rows(h_ref, g_ref, xn_ref, eps):
    @pl.when(pl.program_id(1) == 0)
    def _():
        gain = g_ref[...]

        def body(r, carry):
            rows = pl.ds(pl.multiple_of(r * NORM_ROWS, NORM_ROWS), NORM_ROWS)
            x = h_ref[rows, :]
            ms = jnp.mean(x * x, axis=-1, keepdims=True)
            xn_ref[rows, :] = (x * lax.rsqrt(ms + eps) * gain).astype(xn_ref.dtype)
            return carry

        lax.fori_loop(0, h_ref.shape[0] // NORM_ROWS, body, 0)


def _norm_specs(tm, d):
    return [pl.BlockSpec((tm, d), lambda i, j: (i, 0)), pl.BlockSpec((1, d), lambda i, j: (0, 0))]


def _ffn_up_kernel(h_ref, g_ref, wg_ref, wu_ref, o_ref, xn_ref, *, eps):
    _normalize_rows(h_ref, g_ref, xn_ref, eps)
    x = xn_ref[...]
    g = _dot(x, wg_ref[...])
    u = _dot(x, wu_ref[...])
    o_ref[...] = (g * jax.nn.sigmoid(g) * u).astype(o_ref.dtype)


def _ffn_up(cfg, h, gain, wg, wu, layer, slot):
    m, d = h.shape
    f = wg.shape[-1]
    tm, tn = cfg.tm, cfg.tn
    assert tm % NORM_ROWS == 0
    wspec = pl.BlockSpec((None, None, d, tn), lambda i, j: (layer, slot, 0, j))
    return pl.pallas_call(
        functools.partial(_ffn_up_kernel, eps=EPS),
        out_shape=jax.ShapeDtypeStruct((m, f), BF16),
        grid=(m // tm, f // tn),
        in_specs=_norm_specs(tm, d) + [wspec, wspec],
        out_specs=pl.BlockSpec((tm, tn), lambda i, j: (i, j)),
        scratch_shapes=[pltpu.VMEM((tm, d), BF16)],
        compiler_params=_params("parallel", "arbitrary"),
        name="ffn_up",
    )(h, gain.reshape(1, d), wg, wu)


def _mm_res_kernel(*refs, n_lhs, scale):
    xs, ws = refs[:n_lhs], refs[n_lhs:2 * n_lhs]
    h_ref, o_ref = refs[2 * n_lhs], refs[2 * n_lhs + 1]
    acc = _dot(xs[0][...], ws[0][...])
    for x_ref, w_ref in zip(xs[1:], ws[1:]):
        acc = acc + _dot(x_ref[...], w_ref[...])
    o_ref[...] = h_ref[...] + scale * acc


def _mm_res(cfg, xs, w, w_lead, h, scale):
    m, d = h.shape
    kx = xs[0].shape[1]
    n_lhs = len(xs)
    tm, tn = cfg.tm_res, cfg.tn
    n_lead = len(w_lead)
    w_specs = [
        pl.BlockSpec((None,) * n_lead + (kx, tn), lambda i, j, p=p: tuple(w_lead) + (p, j))
        for p in range(n_lhs)
    ]
    return pl.pallas_call(
        functools.partial(_mm_res_kernel, n_lhs=n_lhs, scale=scale),
        out_shape=jax.ShapeDtypeStruct((m, d), F32),
        grid=(m // tm, d // tn),
        in_specs=[pl.BlockSpec((tm, kx), lambda i, j: (i, 0))] * n_lhs + w_specs
        + [pl.BlockSpec((tm, tn), lambda i, j: (i, j))],
        out_specs=pl.BlockSpec((tm, tn), lambda i, j: (i, j)),
        input_output_aliases={2 * n_lhs: 0},
        compiler_params=_params("parallel", "arbitrary"),
        name="mm_residual",
    )(*xs, *([w] * n_lhs), h)


def _even_inproj_kernel(h_ref, g_ref, w_ref, cos_ref, sin_ref, qg_ref, kg_ref, o_ref, qt_ref,
                        vt_ref, xn_ref, *, n_q_tiles, n_k_tiles, hd, eps, q_scale):
    _normalize_rows(h_ref, g_ref, xn_ref, eps)
    j = pl.program_id(1)
    acc = _dot(xn_ref[...], w_ref[...])
    heads_per_tile = acc.shape[1] // hd

    def normed_rotated(gain):
        c = cos_ref[...]
        s = sin_ref[...]
        lane = lax.broadcasted_iota(jnp.int32, c.shape, 1)
        first_half = (lane % (hd // 2)) < (hd // 4)
        for hh in range(heads_per_tile):
            xh = acc[:, hh * hd:(hh + 1) * hd]
            ms = jnp.mean(xh * xh, axis=-1, keepdims=True)
            y = xh * lax.rsqrt(ms + eps) * gain
            partner = jnp.where(first_half,
                                pltpu.roll(y, hd - hd // 4, axis=1),
                                pltpu.roll(y, hd // 4, axis=1))
            yield hh, y * c + partner * s

    @pl.when(j < n_q_tiles)
    def _():
        for hh, y in normed_rotated(qg_ref[...] * q_scale):
            o_ref[:, hh * hd:(hh + 1) * hd] = y.astype(o_ref.dtype)
            qt_ref[hh * hd:(hh + 1) * hd, :] = y.T.astype(qt_ref.dtype)

    @pl.when((j >= n_q_tiles) & (j < n_q_tiles + n_k_tiles))
    def _():
        for hh, y in normed_rotated(kg_ref[...]):
            o_ref[:, hh * hd:(hh + 1) * hd] = y.astype(o_ref.dtype)

    @pl.when(j >= n_q_tiles + n_k_tiles)
    def _():
        o_ref[...] = acc.astype(o_ref.dtype)

    @pl.when((j >= n_q_tiles + n_k_tiles) & (j < n_q_tiles + 2 * n_k_tiles))
    def _():
        vt_ref[...] = acc.T.astype(vt_ref.dtype)


def _even_inproj(cfg, h, gain, w, layer_i, cos_t, sin_t, q_gain, k_gain):
    m, d = h.shape
    n = w.shape[-1]
    tm, tn, hd = cfg.tm, cfg.tn, cfg.head_dim
    assert cfg.q_w % tn == 0 and cfg.kv_w % tn == 0 and tn % hd == 0 and tm % NORM_ROWS == 0
    n_q, n_k = cfg.q_w // tn, cfg.kv_w // tn
    kern = functools.partial(
        _even_inproj_kernel, n_q_tiles=n_q, n_k_tiles=n_k, hd=hd,
        eps=EPS, q_scale=hd ** -0.5 * math.log2(math.e))
    return pl.pallas_call(
        kern,
        out_shape=(jax.ShapeDtypeStruct((m, n), BF16),
                   jax.ShapeDtypeStruct((cfg.q_w, m), BF16),
                   jax.ShapeDtypeStruct((cfg.kv_w, m), BF16)),
        grid=(m // tm, n // tn),
        in_specs=_norm_specs(tm, d) + [
            pl.BlockSpec((None, d, tn), lambda i, j: (layer_i, 0, j)),
            pl.BlockSpec((tm, hd), lambda i, j: (i, 0)),
            pl.BlockSpec((tm, hd), lambda i, j: (i, 0)),
            pl.BlockSpec((1, hd), lambda i, j: (0, 0)),
            pl.BlockSpec((1, hd), lambda i, j: (0, 0))],
        out_specs=(
            pl.BlockSpec((tm, tn), lambda i, j: (i, j)),
            pl.BlockSpec((tn, tm), lambda i, j: (jnp.minimum(j, n_q - 1), i)),
            pl.BlockSpec((tn, tm), lambda i, j: (jnp.clip(j - n_q - n_k, 0, n_k - 1), i))),
        scratch_shapes=[pltpu.VMEM((tm, d), BF16)],
        compiler_params=_params("parallel", "arbitrary"),
        name="even_inproj",
    )(h, gain.reshape(1, d), w, cos_t, sin_t, q_gain.reshape(1, hd), k_gain.reshape(1, hd))


def _attn_kernel(qt_ref, k_ref, vt_ref, o_ref, acc_ref, *, groups, hpc, hd, head, sub, n_sub,
                 n_pad, unroll):
    tq = qt_ref.shape[1]
    n_chain = groups // hpc
    cw = hpc * tq
    q_t = [jnp.concatenate([qt_ref[(c * hpc + g) * hd:(c * hpc + g + 1) * hd, :]
                            for g in range(hpc)], axis=1) for c in range(n_chain)]

    def scores(start, size):
        k_blk = k_ref[pl.ds(start, size), :]
        return tuple(_dot(k_blk, q_t[c]) for c in range(n_chain))

    def consume(start, size, st, ml, masked):
        v_blk = vt_ref[:, pl.ds(start, size)]
        out = []
        for c in range(n_chain):
            m_prev, l_prev = ml[c]
            s = st[c]
            if masked:
                row = lax.broadcasted_iota(jnp.int32, s.shape, 0)
                s = jnp.where(row < n_pad, NEG_BIG, s)
            m_new = jnp.maximum(m_prev, jnp.max(s, axis=0, keepdims=True))
            alpha = jnp.exp2(m_prev - m_new)
            p = jnp.exp2(s - m_new)
            l_new = alpha * l_prev + jnp.sum(p, axis=0, keepdims=True)
            acc_ref[c] = alpha * acc_ref[c] + _dot(v_blk, p.astype(v_blk.dtype))
            out.append((m_new, l_new))
        return tuple(out)

    acc_ref[...] = jnp.zeros_like(acc_ref)
    ml = tuple((jnp.full((1, cw), NEG_BIG, F32), jnp.zeros((1, cw), F32))
               for _ in range(n_chain))
    def sub_start(j):
        return pl.multiple_of(head + j * sub, LANES)

    st = scores(0, head)
    st_next = scores(sub_start(0), sub)
    ml = consume(0, head, st, ml, True)

    def body(it, carry):
        st, ml = carry
        for u in range(unroll):
            j = it * unroll + u
            st_next = scores(sub_start(jnp.minimum(j + 1, n_sub - 1)), sub)
            ml = consume(sub_start(j), sub, st, ml, False)
            st = st_next
        return st, ml

    _, ml = lax.fori_loop(0, n_sub // unroll, body, (st_next, ml))
    for c in range(n_chain):
        o_t = acc_ref[c] / ml[c][1]
        for g in range(hpc):
            h0 = (c * hpc + g) * hd
            o_ref[:, h0:h0 + hd] = o_t[:, g * tq:(g + 1) * tq].T.astype(o_ref.dtype)


def _attention(cfg, u, qt, vt):
    m = u.shape[0]
    hd, kv, tq, lp = cfg.head_dim, cfg.attn_kv_heads, cfg.q_block, cfg.lp
    groups = cfg.attn_heads // kv
    hpc = min(groups, 2)
    n_q = lp // tq
    sub = cfg.attn_tk
    head = -(-cfg.n_pad // LANES) * LANES if cfg.n_pad else LANES
    n_sub = (lp - head) // sub
    assert (lp - head) % sub == 0 and n_sub % cfg.attn_unroll == 0 and sub % LANES == 0
    kern = functools.partial(_attn_kernel, groups=groups, hpc=hpc, hd=hd, head=head, sub=sub,
                             n_sub=n_sub, n_pad=cfg.n_pad, unroll=cfg.attn_unroll)
    k_col0 = cfg.q_w // hd
    return pl.pallas_call(
        kern,
        out_shape=jax.ShapeDtypeStruct((m, cfg.q_w), BF16),
        grid=(cfg.batch, kv, n_q),
        in_specs=[pl.BlockSpec((groups * hd, tq), lambda b, h, i: (h, b * n_q + i)),
                  pl.BlockSpec((lp, hd), lambda b, h, i: (b, k_col0 + h)),
                  pl.BlockSpec((hd, lp), lambda b, h, i: (h, b))],
        out_specs=pl.BlockSpec((tq, groups * hd), lambda b, h, i: (b * n_q + i, h)),
        scratch_shapes=[pltpu.VMEM((groups // hpc, hd, hpc * tq), F32)],
        compiler_params=_params("parallel", "parallel", "arbitrary"),
        name="gqa_attention",
    )(qt, u, vt)


def _chan_dft_kernel(f_ref, cs_ref, a_ref, b_ref, *, gd):
    cs = cs_ref[...]
    for g in range(f_ref.shape[1] // gd):
        ab = _dot(f_ref[:, g * gd:(g + 1) * gd], cs)
        a_ref[:, g * gd:(g + 1) * gd] = ab[:, :gd].astype(a_ref.dtype)
        b_ref[:, g * gd:(g + 1) * gd] = ab[:, gd:].astype(b_ref.dtype)


def _chan_dft(cfg, u, cs):
    m = u.shape[0]
    gd, fw = cfg.four_dim, cfg.four_w
    f0 = cfg.q_w + 2 * cfg.kv_w
    fb = math.gcd(math.gcd(f0, fw), 1024)
    tm = cfg.tm
    out = jax.ShapeDtypeStruct((m, fw), BF16)
    return pl.pallas_call(
        functools.partial(_chan_dft_kernel, gd=gd),
        out_shape=(out, out),
        grid=(m // tm, fw // fb),
        in_specs=[pl.BlockSpec((tm, fb), lambda i, j: (i, f0 // fb + j)),
                  pl.BlockSpec((gd, 2 * gd), lambda i, j: (0, 0))],
        out_specs=(pl.BlockSpec((tm, fb), lambda i, j: (i, j)),
                   pl.BlockSpec((tm, fb), lambda i, j: (i, j))),
        compiler_params=_params("parallel", "parallel"),
        name="fourier_channel_dft",
    )(u, cs)


def _seq_dft_kernel(wc_ref, ws_ref, a_ref, b_ref, o_ref, acc_ref):
    k = pl.program_id(3)

    @pl.when(k == 0)
    def _():
        acc_ref[...] = jnp.zeros_like(acc_ref)

    acc_ref[...] += _dot(wc_ref[...], a_ref[...]) - _dot(ws_ref[...], b_ref[...])

    @pl.when(k == pl.num_programs(3) - 1)
    def _():
        o_ref[...] = acc_ref[...].astype(o_ref.dtype)


def _seq_dft(cfg, wc, ws, a, b):
    m, fw = a.shape
    lp = cfg.lp
    tm, tn, tk = cfg.dft_tm, cfg.dft_tn, cfg.dft_tk
    n_m, n_k = lp // tm, lp // tk
    return pl.pallas_call(
        _seq_dft_kernel,
        out_shape=jax.ShapeDtypeStruct((m, fw), BF16),
        grid=(cfg.batch, fw // tn, n_m, n_k),
        in_specs=[pl.BlockSpec((tm, tk), lambda bb, n, i, k: (i, k)),
                  pl.BlockSpec((tm, tk), lambda bb, n, i, k: (i, k)),
                  pl.BlockSpec((tk, tn), lambda bb, n, i, k: (bb * n_k + k, n)),
                  pl.BlockSpec((tk, tn), lambda bb, n, i, k: (bb * n_k + k, n))],
        out_specs=pl.BlockSpec((tm, tn), lambda bb, n, i, k: (bb * n_m + i, n)),
        scratch_shapes=[pltpu.VMEM((tm, tn), F32)],
        compiler_params=_params("parallel", "parallel", "parallel", "arbitrary"),
        name="fourier_sequence_dft",
    )(wc, ws, a, b)


def _odd_inproj_kernel(h_ref, g_ref, w_ref, ga_ref, o_ref, low_ref, xn_ref, *, n_q_tiles,
                       q_scale, eps):
    _normalize_rows(h_ref, g_ref, xn_ref, eps)
    x = xn_ref[...]

    @pl.when(pl.program_id(1) == 0)
    def _():
        low_ref[...] = _dot(x, ga_ref[...])

    acc = _dot(x, w_ref[...])
    sc = jnp.where(pl.program_id(1) < n_q_tiles, q_scale, 1.0).astype(F32)
    o_ref[...] = (acc * sc).astype(o_ref.dtype)


def _odd_inproj(cfg, h, gain, w, layer_i, ga):
    m, d = h.shape
    n = w.shape[-1]
    r2 = ga.shape[-1]
    tm, tn = cfg.tm, cfg.tn
    assert tm % NORM_ROWS == 0
    kern = functools.partial(_odd_inproj_kernel, n_q_tiles=cfg.gla_heads * cfg.gla_dk // tn,
                             q_scale=cfg.gla_dk ** -0.5, eps=EPS)
    return pl.pallas_call(
        kern,
        out_shape=(jax.ShapeDtypeStruct((m, n), BF16), jax.ShapeDtypeStruct((m, r2), F32)),
        grid=(m // tm, n // tn),
        in_specs=_norm_specs(tm, d) + [
            pl.BlockSpec((None, d, tn), lambda i, j: (layer_i, 0, j)),
            pl.BlockSpec((None, d, r2), lambda i, j: (layer_i, 0, 0))],
        out_specs=(pl.BlockSpec((tm, tn), lambda i, j: (i, j)),
                   pl.BlockSpec((tm, r2), lambda i, j: (i, 0))),
        scratch_shapes=[pltpu.VMEM((tm, d), BF16)],
        compiler_params=_params("parallel", "arbitrary"),
        name="odd_inproj",
    )(h, gain.reshape(1, d), w, ga)


def _gla_kernel(*refs, rev, chunk, sub, halo, hpb, dk, dv, n_chunks, n_pad, tau, eps):
    if rev:
        (q_ref, k_ref, v_ref, low_ref, gb_ref, bias_ref, tri_ref, lmask_ref,
         ofw_ref, r_ref, hn_ref, o_ref, s_ref, cump_ref, kfp_ref) = refs
    else:
        (q_ref, k_ref, v_ref, low_ref, gb_ref, bias_ref, tri_ref, lmask_ref,
         o_ref, s_ref, cump_ref, kfp_ref) = refs
    c = pl.program_id(2)
    pos_chunk = (n_chunks - 1 - c) if rev else c

    @pl.when(c == 0)
    def _():
        s_ref[...] = jnp.zeros_like(s_ref)
        cump_ref[...] = jnp.zeros_like(cump_ref)
        kfp_ref[...] = jnp.zeros_like(kfp_ref)

    row = pos_chunk * chunk + lax.broadcasted_iota(jnp.int32, (chunk, 1), 0)
    valid = row >= n_pad
    col_minus_row = (lax.broadcasted_iota(jnp.int32, (chunk, chunk), 1)
                     - lax.broadcasted_iota(jnp.int32, (chunk, chunk), 0))
    row_in_sub = lax.broadcasted_iota(jnp.int32, (chunk, 1), 0) % sub
    low = low_ref[...]
    tri = tri_ref[...]

    def head(hx):
        ks = slice(hx * dk, (hx + 1) * dk)
        vs = slice(hx * dv, (hx + 1) * dv)
        z = jnp.dot(low, gb_ref[:, ks], preferred_element_type=F32,
                    precision=lax.Precision.HIGHEST) + bias_ref[:, ks]
        log_a = (jnp.minimum(z, 0.0) - jnp.log(1.0 + jnp.exp(-jnp.abs(z)))) * (1.0 / tau)
        log_a = jnp.where(valid, log_a, 0.0)
        kf = jnp.where(valid, k_ref[:, ks].astype(F32), 0.0)
        qf = q_ref[:, ks].astype(F32)
        v = v_ref[:, vs]

        g_hi = log_a.astype(BF16)
        g_lo = (log_a - g_hi.astype(F32)).astype(BF16)
        cum = _dot(tri, g_hi) + _dot(tri, g_lo)
        cum_end = cum[0:1] if rev else cum[chunk - 1:chunk]
        yield

        s_old = s_ref[hx]
        inter = _dot_nt((qf * jnp.exp(cum)).astype(BF16), s_old.astype(BF16))
        yield

        a_acc = jnp.zeros((chunk, chunk), F32)
        s_blk, level = chunk // 2, 0
        while s_blk >= sub:
            grp = 2 * s_blk
            ref_idx = s_blk if rev else s_blk - 1
            ref = jnp.broadcast_to(
                cum.reshape(chunk // grp, grp, dk)[:, ref_idx:ref_idx + 1, :],
                (chunk // grp, grp, dk)).reshape(chunk, dk)
            e = jnp.exp(-jnp.abs(cum - ref))
            a_lvl = _dot_nt((qf * e).astype(BF16), (kf * e).astype(BF16))
            a_acc = a_acc + a_lvl * lmask_ref[level]
            s_blk //= 2
            level += 1
            yield

        cump_ref[hx, halo:halo + chunk, :] = cum
        kfp_ref[hx, halo:halo + chunk, :] = kf
        a_acc = a_acc + jnp.where(col_minus_row == 0,
                                  jnp.sum(qf * kf, axis=-1, keepdims=True), 0.0)
        for delta in range(1, sub):
            off = halo + delta if rev else halo - delta
            k_sh = kfp_ref[hx, off:off + chunk, :]
            cum_sh = cump_ref[hx, off:off + chunk, :]
            col = jnp.sum(qf * k_sh * jnp.exp(cum - cum_sh), axis=-1, keepdims=True)
            same_sub = (row_in_sub + delta < sub) if rev else (row_in_sub >= delta)
            col = jnp.where(same_sub, col, 0.0)
            a_acc = a_acc + jnp.where(col_minus_row == (delta if rev else -delta), col, 0.0)
            yield

        o = inter + _dot(a_acc.astype(BF16), v)

        k_out = (kf * jnp.exp(cum_end - cum)).astype(BF16)
        s_ref[hx] = s_old * jnp.exp(cum_end) + _dot_tn(v, k_out)
        yield

        if rev:
            tot = ofw_ref[:, vs] + o
            ms = jnp.mean(tot * tot, axis=-1, keepdims=True)
            y = tot * lax.rsqrt(ms + eps) * hn_ref[...]
            r = r_ref[:, vs].astype(F32)
            o_ref[:, vs] = (y * r * jax.nn.sigmoid(r)).astype(o_ref.dtype)
        else:
            o_ref[:, vs] = o

    chains = [head(hx) for hx in range(hpb)]
    while chains:
        for g in list(chains):
            if next(g, "done") == "done":
                chains.remove(g)


def _gla_constants(cfg, rev):
    c, sub = cfg.gla_chunk, cfg.gla_sub
    i = jnp.arange(c)[:, None]
    j = jnp.arange(c)[None, :]
    tri = ((j >= i) if rev else (j <= i)).astype(BF16)
    masks = []
    s = c // 2
    while s >= sub:
        i_hi, j_hi = (i // s) % 2 == 1, (j // s) % 2 == 1
        roles = (~i_hi & j_hi) if rev else (i_hi & ~j_hi)
        masks.append(((i // (2 * s) == j // (2 * s)) & roles).astype(F32))
        s //= 2
    return tri, jnp.stack(masks)


def _gla(cfg, u, low, gate_b, gate_bias, rev, o_fw=None, head_norm=None):
    m = u.shape[0]
    hh, dk, dv, c = cfg.gla_heads, cfg.gla_dk, cfg.gla_dv, cfg.gla_chunk
    rank = cfg.gla_rank
    n_chunks = cfg.lp // c
    tri, lmask = _gla_constants(cfg, rev)
    n_lvl = lmask.shape[0]
    hpb = cfg.gla_heads_per_step
    kw, vw = hpb * dk, hpb * dv
    assert hh % hpb == 0 and (2 * hh * dk) % vw == 0
    k_col0 = hh // hpb
    v_col0 = 2 * hh * dk // vw
    r_col0 = v_col0 + hh // hpb

    def rowblk(b, h, ci):
        return b * n_chunks + ((n_chunks - 1 - ci) if rev else ci)

    in_specs = [
        pl.BlockSpec((c, kw), lambda b, h, ci: (rowblk(b, h, ci), h)),
        pl.BlockSpec((c, kw), lambda b, h, ci: (rowblk(b, h, ci), k_col0 + h)),
        pl.BlockSpec((c, vw), lambda b, h, ci: (rowblk(b, h, ci), v_col0 + h)),
        pl.BlockSpec((c, rank), lambda b, h, ci: (rowblk(b, h, ci), 0)),
        pl.BlockSpec((rank, kw), lambda b, h, ci: (0, h)),
        pl.BlockSpec((1, kw), lambda b, h, ci: (0, h)),
        pl.BlockSpec((c, c), lambda b, h, ci: (0, 0)),
        pl.BlockSpec((n_lvl, c, c), lambda b, h, ci: (0, 0, 0)),
    ]
    args = [u, u, u, low, gate_b, gate_bias.reshape(1, -1), tri, lmask]
    if rev:
        in_specs += [
            pl.BlockSpec((c, vw), lambda b, h, ci: (rowblk(b, h, ci), h)),
            pl.BlockSpec((c, vw), lambda b, h, ci: (rowblk(b, h, ci), r_col0 + h)),
            pl.BlockSpec((1, dv), lambda b, h, ci: (0, 0)),
        ]
        args += [o_fw, u, head_norm.reshape(1, dv)]
    halo = -(-cfg.gla_sub // SUBLANES) * SUBLANES
    kern = functools.partial(_gla_kernel, rev=rev, chunk=c, sub=cfg.gla_sub, halo=halo,
                             hpb=hpb, dk=dk, dv=dv, n_chunks=n_chunks, n_pad=cfg.n_pad,
                             tau=GLA_GATE_TAU, eps=EPS)
    return pl.pallas_call(
        kern,
        out_shape=jax.ShapeDtypeStruct((m, hh * dv), BF16 if rev else F32),
        grid=(cfg.batch, hh // hpb, n_chunks),
        in_specs=in_specs,
        out_specs=pl.BlockSpec((c, vw), lambda b, h, ci: (rowblk(b, h, ci), h)),
        scratch_shapes=[pltpu.VMEM((hpb, dv, dk), F32),
                        pltpu.VMEM((hpb, c + 2 * halo, dk), F32),
                        pltpu.VMEM((hpb, c + 2 * halo, dk), F32)],
        compiler_params=_params("parallel", "parallel", "arbitrary"),
        name="gla_reverse" if rev else "gla_forward",
    )(*args)


def _rope_tables(cfg):
    hd = cfg.head_dim
    n_freq = hd // 4
    t = jnp.arange(cfg.seq)
    rows = (t // cfg.grid_w).astype(F32)
    cols = (t % cfg.grid_w).astype(F32)
    inv_freq = jnp.power(ROPE_THETA, -jnp.arange(n_freq, dtype=F32) / n_freq)
    ang = jnp.concatenate([jnp.tile(rows[:, None] * inv_freq, (1, 2)),
                           jnp.tile(cols[:, None] * inv_freq, (1, 2))], axis=1)
    ang = jnp.concatenate([jnp.zeros((cfg.n_pad + cfg.n_meta, hd), F32), ang], axis=0)
    sign = jnp.where((jnp.arange(hd) % (hd // 2)) < n_freq, -1.0, 1.0).astype(F32)
    cos_t = jnp.tile(jnp.cos(ang), (cfg.batch, 1))
    sin_t = jnp.tile(jnp.sin(ang) * sign, (cfg.batch, 1))
    return cos_t, sin_t


def _dft_tables(cfg):
    ll, gd, lp, n_pad = cfg.length, cfg.four_dim, cfg.lp, cfg.n_pad
    scale = 1.0 / math.sqrt(ll * gd)
    n_hi = -(-ll // LANES)
    t = jnp.arange(ll, dtype=jnp.int32)[None, :]
    k_lo = jnp.arange(LANES, dtype=jnp.int32)[:, None]
    k_hi = jnp.arange(n_hi, dtype=jnp.int32)[:, None] * LANES
    ang_lo = (2.0 * math.pi / ll) * ((k_lo * t) % ll).astype(F32)
    ang_hi = (2.0 * math.pi / ll) * ((k_hi * t) % ll).astype(F32)
    c_lo, s_lo = jnp.cos(ang_lo)[None], jnp.sin(ang_lo)[None]
    c_hi, s_hi = (jnp.cos(ang_hi) * scale)[:, None], (jnp.sin(ang_hi) * scale)[:, None]
    wc = (c_hi * c_lo - s_hi * s_lo).reshape(n_hi * LANES, ll)[:ll]
    ws = (s_hi * c_lo + c_hi * s_lo).reshape(n_hi * LANES, ll)[:ll]
    pad = ((n_pad, lp - n_pad - ll), (n_pad, lp - n_pad - ll))
    wc = jnp.pad(wc.astype(BF16), pad)
    ws = jnp.pad(ws.astype(BF16), pad)
    cc = jnp.arange(gd, dtype=jnp.int32)
    ang_c = (2.0 * math.pi / gd) * ((cc[:, None] * cc[None, :]) % gd).astype(F32)
    cs = jnp.concatenate([jnp.cos(ang_c), jnp.sin(ang_c)], axis=1).astype(BF16)
    return wc, ws, cs


def _forward(cfg, x, meta_tokens, pre_norm, ffn_w_gate, ffn_w_up, ffn_w_down, even_w_in,
             even_q_norm, even_k_norm, even_w_out, odd_w_in, odd_gate_a, odd_gate_b,
             odd_gate_bias, odd_head_norm, odd_w_out):
    bsz, d = cfg.batch, cfg.d_model
    meta = jnp.broadcast_to(meta_tokens.astype(F32)[None], (bsz, cfg.n_meta, d))
    h = jnp.concatenate([jnp.zeros((bsz, cfg.n_pad, d), F32), meta, x.astype(F32)], axis=1)
    h = h.reshape(cfg.rows, d)

    wg, wu, wd = ffn_w_gate.astype(BF16), ffn_w_up.astype(BF16), ffn_w_down.astype(BF16)
    ew_in, ew_out = even_w_in.astype(BF16), even_w_out.astype(BF16)
    ow_in, ow_out = odd_w_in.astype(BF16), odd_w_out.astype(BF16)
    ga = jnp.concatenate([odd_gate_a[:, 0], odd_gate_a[:, 1]], axis=-1).astype(BF16)

    cos_t, sin_t = _rope_tables(cfg)
    wc, ws, cs = _dft_tables(cfg)

    def ffn(h, layer, slot, norm_slot):
        a = _ffn_up(cfg, h, pre_norm[layer, norm_slot], wg, wu, layer, slot)
        return _mm_res(cfg, [a], wd, (layer, slot), h, 0.5)

    for layer in range(cfg.depth):
        i = layer // 2
        h = ffn(h, layer, 0, 0)
        mix_gain = pre_norm[layer, 1]
        if layer % 2 == 0:
            u, qt, vt = _even_inproj(cfg, h, mix_gain, ew_in, i, cos_t, sin_t, even_q_norm[i],
                                     even_k_norm[i])
            attn = _attention(cfg, u, qt, vt)
            fa, fb = _chan_dft(cfg, u, cs)
            four = _seq_dft(cfg, wc, ws, fa, fb)
            h = _mm_res(cfg, [attn, four], ew_out, (i,), h, 1.0)
        else:
            u, low = _odd_inproj(cfg, h, mix_gain, ow_in, i, ga)
            rank = cfg.gla_rank
            o_fw = _gla(cfg, u, low[:, :rank], odd_gate_b[i, 0], odd_gate_bias[i, 0], rev=False)
            og = _gla(cfg, u, low[:, rank:], odd_gate_b[i, 1], odd_gate_bias[i, 1], rev=True,
                      o_fw=o_fw, head_norm=odd_head_norm[i])
            h = _mm_res(cfg, [og], ow_out, (i,), h, 1.0)
        h = ffn(h, layer, 1, 2)
    return h.reshape(bsz, cfg.lp, d)[:, cfg.n_pad + cfg.n_meta:]


def kernel(x, meta_tokens, pre_norm, ffn_w_gate, ffn_w_up, ffn_w_down, even_w_in, even_q_norm,
           even_k_norm, even_w_out, odd_w_in, odd_gate_a, odd_gate_b, odd_gate_bias,
           odd_head_norm, odd_w_out):
    bsz, seq, d = x.shape
    cfg = Cfg(batch=bsz, seq=seq, d_model=d, d_ff=ffn_w_gate.shape[-1], depth=pre_norm.shape[0],
              n_meta=N_META, grid_w=GRID_W, attn_heads=ATTN_HEADS, attn_kv_heads=ATTN_KV_HEADS,
              head_dim=HEAD_DIM, q_block=Q_BLOCK, four_groups=FOURIER_GROUPS,
              four_dim=FOURIER_GROUP_DIM, gla_heads=GLA_HEADS, gla_rank=odd_gate_a.shape[-1],
              tm_res=1280, tm=640, tn=512, attn_tk=512, attn_unroll=4, dft_tm=1664, dft_tn=1024, dft_tk=640,
              gla_chunk=128, gla_sub=4, gla_heads_per_step=4)
    return _forward(cfg, x, meta_tokens, pre_norm, ffn_w_gate, ffn_w_up, ffn_w_down, even_w_in,
                    even_q_norm, even_k_norm, even_w_out, odd_w_in, odd_gate_a, odd_gate_b,
                    odd_gate_bias, odd_head_norm, odd_w_out)
```

```python
import functools
import math
from typing import NamedTuple

import jax
import jax.numpy as jnp
from jax import lax
from jax.experimental import pallas as pl
from jax.experimental.pallas import tpu as pltpu

F32 = jnp.float32
BF16 = jnp.bfloat16

N_META = 16
GRID_W = 64
EPS = 1e-6
ATTN_HEADS = 16
ATTN_KV_HEADS = 4
HEAD_DIM = 128
Q_BLOCK = 128
ROPE_THETA = 10000.0
FOURIER_GROUPS = 16
FOURIER_GROUP_DIM = 128
GLA_HEADS = 8
GLA_GATE_TAU = 16.0

V7X_VMEM_BYTES = 64 * 1024 * 1024
VMEM_LIMIT_BYTES = V7X_VMEM_BYTES - 8 * 1024 * 1024
LANES = 128
SUBLANES = 8
NEG_BIG = -1e30


class Cfg(NamedTuple):
    batch: int
    seq: int
    d_model: int
    d_ff: int
    depth: int
    n_meta: int
    grid_w: int
    attn_heads: int
    attn_kv_heads: int
    head_dim: int
    q_block: int
    four_groups: int
    four_dim: int
    gla_heads: int
    gla_rank: int
    tm_res: int
    tm: int
    tn: int
    attn_tk: int
    attn_unroll: int
    dft_tm: int
    dft_tn: int
    dft_tk: int
    gla_chunk: int
    gla_sub: int
    gla_heads_per_step: int

    @property
    def length(self):
        return self.n_meta + self.seq

    @property
    def n_pad(self):
        return (-self.length) % self.q_block

    @property
    def lp(self):
        return self.length + self.n_pad

    @property
    def rows(self):
        return self.batch * self.lp

    @property
    def q_w(self):
        return self.attn_heads * self.head_dim

    @property
    def kv_w(self):
        return self.attn_kv_heads * self.head_dim

    @property
    def four_w(self):
        return self.four_groups * self.four_dim

    @property
    def gla_dk(self):
        return self.d_model // (2 * self.gla_heads)

    @property
    def gla_dv(self):
        return self.d_model // self.gla_heads


def _params(*sem):
    return pltpu.CompilerParams(dimension_semantics=sem, vmem_limit_bytes=VMEM_LIMIT_BYTES)


def _dot(a, b):
    return jnp.dot(a, b, preferred_element_type=F32)


def _dot_nt(a, b):
    return lax.dot_general(a, b, (((1,), (1,)), ((), ())), preferred_element_type=F32)


def _dot_tn(a, b):
    return lax.dot_general(a, b, (((0,), (0,)), ((), ())), preferred_element_type=F32)


NORM_ROWS = 16
NORM_UNROLL = 8


def _normalize_rows(h_ref, g_ref, xn_ref, eps):
    @pl.when(pl.program_id(1) == 0)
    def _():
        gain = g_ref[...]

        def body(r, carry):
            rows = pl.ds(pl.multiple_of(r * NORM_ROWS, NORM_ROWS), NORM_ROWS)
            x = h_ref[rows, :]
            ms = jnp.mean(x * x, axis=-1, keepdims=True)
            xn_ref[rows, :] = (x * lax.rsqrt(ms + eps) * gain).astype(xn_ref.dtype)
            return carry

        lax.fori_loop(0, h_ref.shape[0] // NORM_ROWS, body, 0, unroll=NORM_UNROLL)


def _norm_specs(tm, d):
    return [pl.BlockSpec((tm, d), lambda i, j: (i, 0)), pl.BlockSpec((1, d), lambda i, j: (0, 0))]


def _ffn_up_kernel(h_ref, g_ref, wg_ref, wu_ref, o_ref, xn_ref, *, eps):
    _normalize_rows(h_ref, g_ref, xn_ref, eps)
    x = xn_ref[...]
    g = _dot(x, wg_ref[...])
    u = _dot(x, wu_ref[...])
    o_ref[...] = (g * jax.nn.sigmoid(g) * u).astype(o_ref.dtype)


def _ffn_up(cfg, h, gain, wg, wu, layer, slot):
    m, d = h.shape
    f = wg.shape[-1]
    tm, tn = cfg.tm, cfg.tn
    assert tm % NORM_ROWS == 0
    wspec = pl.BlockSpec((None, None, d, tn), lambda i, j: (layer, slot, 0, j))
    return pl.pallas_call(
        functools.partial(_ffn_up_kernel, eps=EPS),
        out_shape=jax.ShapeDtypeStruct((m, f), BF16),
        grid=(m // tm, f // tn),
        in_specs=_norm_specs(tm, d) + [wspec, wspec],
        out_specs=pl.BlockSpec((tm, tn), lambda i, j: (i, j)),
        scratch_shapes=[pltpu.VMEM((tm, d), BF16)],
        compiler_params=_params("parallel", "arbitrary"),
        name="ffn_up",
    )(h, gain.reshape(1, d), wg, wu)


def _mm_res_kernel(*refs, n_lhs, scale):
    xs, ws = refs[:n_lhs], refs[n_lhs:2 * n_lhs]
    h_ref, o_ref = refs[2 * n_lhs], refs[2 * n_lhs + 1]
    acc = _dot(xs[0][...], ws[0][...])
    for x_ref, w_ref in zip(xs[1:], ws[1:]):
        acc = acc + _dot(x_ref[...], w_ref[...])
    o_ref[...] = h_ref[...] + scale * acc


def _mm_res(cfg, xs, w, w_lead, h, scale):
    m, d = h.shape
    kx = xs[0].shape[1]
    n_lhs = len(xs)
    tm, tn = cfg.tm_res, cfg.tn
    n_lead = len(w_lead)
    w_specs = [
        pl.BlockSpec((None,) * n_lead + (kx, tn), lambda i, j, p=p: tuple(w_lead) + (p, j))
        for p in range(n_lhs)
    ]
    return pl.pallas_call(
        functools.partial(_mm_res_kernel, n_lhs=n_lhs, scale=scale),
        out_shape=jax.ShapeDtypeStruct((m, d), F32),
        grid=(m // tm, d // tn),
        in_specs=[pl.BlockSpec((tm, kx), lambda i, j: (i, 0))] * n_lhs + w_specs
        + [pl.BlockSpec((tm, tn), lambda i, j: (i, j))],
        out_specs=pl.BlockSpec((tm, tn), lambda i, j: (i, j)),
        input_output_aliases={2 * n_lhs: 0},
        compiler_params=_params("parallel", "arbitrary"),
        name="mm_residual",
    )(*xs, *([w] * n_lhs), h)


def _even_inproj_kernel(h_ref, g_ref, w_ref, cos_ref, sin_ref, qg_ref, kg_ref, o_ref, qt_ref,
                        vt_ref, xn_ref, *, n_q_tiles, n_k_tiles, hd, eps, q_scale):
    _normalize_rows(h_ref, g_ref, xn_ref, eps)
    j = pl.program_id(1)
    acc = _dot(xn_ref[...], w_ref[...])
    heads_per_tile = acc.shape[1] // hd

    def normed_rotated(gain):
        c = cos_ref[...]
        s = sin_ref[...]
        lane = lax.broadcasted_iota(jnp.int32, c.shape, 1)
        first_half = (lane % (hd // 2)) < (hd // 4)
        for hh in range(heads_per_tile):
            xh = acc[:, hh * hd:(hh + 1) * hd]
            ms = jnp.mean(xh * xh, axis=-1, keepdims=True)
            y = xh * lax.rsqrt(ms + eps) * gain
            partner = jnp.where(first_half,
                                pltpu.roll(y, hd - hd // 4, axis=1),
                                pltpu.roll(y, hd // 4, axis=1))
            yield hh, y * c + partner * s

    @pl.when(j < n_q_tiles)
    def _():
        for hh, y in normed_rotated(qg_ref[...] * q_scale):
            o_ref[:, hh * hd:(hh + 1) * hd] = y.astype(o_ref.dtype)
            qt_ref[hh * hd:(hh + 1) * hd, :] = y.T.astype(qt_ref.dtype)

    @pl.when((j >= n_q_tiles) & (j < n_q_tiles + n_k_tiles))
    def _():
        for hh, y in normed_rotated(kg_ref[...]):
            o_ref[:, hh * hd:(hh + 1) * hd] = y.astype(o_ref.dtype)

    @pl.when(j >= n_q_tiles + n_k_tiles)
    def _():
        o_ref[...] = acc.astype(o_ref.dtype)

    @pl.when((j >= n_q_tiles + n_k_tiles) & (j < n_q_tiles + 2 * n_k_tiles))
    def _():
        vt_ref[...] = acc.T.astype(vt_ref.dtype)


def _even_inproj(cfg, h, gain, w, layer_i, cos_t, sin_t, q_gain, k_gain):
    m, d = h.shape
    n = w.shape[-1]
    tm, tn, hd = cfg.tm, cfg.tn, cfg.head_dim
    assert cfg.q_w % tn == 0 and cfg.kv_w % tn == 0 and tn % hd == 0 and tm % NORM_ROWS == 0
    n_q, n_k = cfg.q_w // tn, cfg.kv_w // tn
    kern = functools.partial(
        _even_inproj_kernel, n_q_tiles=n_q, n_k_tiles=n_k, hd=hd,
        eps=EPS, q_scale=hd ** -0.5 * math.log2(math.e))
    return pl.pallas_call(
        kern,
        out_shape=(jax.ShapeDtypeStruct((m, n), BF16),
                   jax.ShapeDtypeStruct((cfg.q_w, m), BF16),
                   jax.ShapeDtypeStruct((cfg.kv_w, m), BF16)),
        grid=(m // tm, n // tn),
        in_specs=_norm_specs(tm, d) + [
            pl.BlockSpec((None, d, tn), lambda i, j: (layer_i, 0, j)),
            pl.BlockSpec((tm, hd), lambda i, j: (i, 0)),
            pl.BlockSpec((tm, hd), lambda i, j: (i, 0)),
            pl.BlockSpec((1, hd), lambda i, j: (0, 0)),
            pl.BlockSpec((1, hd), lambda i, j: (0, 0))],
        out_specs=(
            pl.BlockSpec((tm, tn), lambda i, j: (i, j)),
            pl.BlockSpec((tn, tm), lambda i, j: (jnp.minimum(j, n_q - 1), i)),
            pl.BlockSpec((tn, tm), lambda i, j: (jnp.clip(j - n_q - n_k, 0, n_k - 1), i))),
        scratch_shapes=[pltpu.VMEM((tm, d), BF16)],
        compiler_params=_params("parallel", "arbitrary"),
        name="even_inproj",
    )(h, gain.reshape(1, d), w, cos_t, sin_t, q_gain.reshape(1, hd), k_gain.reshape(1, hd))


def _attn_kernel(qt_ref, k_ref, vt_ref, o_ref, acc_ref, *, groups, hpc, hd, head, sub, n_sub,
                 n_pad, unroll):
    tq = qt_ref.shape[1]
    n_chain = groups // hpc
    cw = hpc * tq
    q_t = [jnp.concatenate([qt_ref[(c * hpc + g) * hd:(c * hpc + g + 1) * hd, :]
                            for g in range(hpc)], axis=1) for c in range(n_chain)]

    def scores(start, size):
        k_blk = k_ref[pl.ds(start, size), :]
        return tuple(_dot(k_blk, q_t[c]) for c in range(n_chain))

    def consume(start, size, st, ml, masked):
        v_blk = vt_ref[:, pl.ds(start, size)]
        out = []
        for c in range(n_chain):
            m_prev, l_prev = ml[c]
            s = st[c]
            if masked:
                row = lax.broadcasted_iota(jnp.int32, s.shape, 0)
                s = jnp.where(row < n_pad, NEG_BIG, s)
            m_new = jnp.maximum(m_prev, jnp.max(s, axis=0, keepdims=True))
            alpha = jnp.exp2(m_prev - m_new)
            p = jnp.exp2(s - m_new)
            l_new = alpha * l_prev + jnp.sum(p, axis=0, keepdims=True)
            acc_ref[c] = alpha * acc_ref[c] + _dot(v_blk, p.astype(v_blk.dtype))
            out.append((m_new, l_new))
        return tuple(out)

    acc_ref[...] = jnp.zeros_like(acc_ref)
    ml = tuple((jnp.full((1, cw), NEG_BIG, F32), jnp.zeros((1, cw), F32))
               for _ in range(n_chain))
    def sub_start(j):
        return pl.multiple_of(head + j * sub, LANES)

    st = scores(0, head)
    st_next = scores(sub_start(0), sub)
    ml = consume(0, head, st, ml, True)

    def body(it, carry):
        st, ml = carry
        for u in range(unroll):
            j = it * unroll + u
            st_next = scores(sub_start(jnp.minimum(j + 1, n_sub - 1)), sub)
            ml = consume(sub_start(j), sub, st, ml, False)
            st = st_next
        return st, ml

    _, ml = lax.fori_loop(0, n_sub // unroll, body, (st_next, ml))
    for c in range(n_chain):
        o_t = acc_ref[c] / ml[c][1]
        for g in range(hpc):
            h0 = (c * hpc + g) * hd
            o_ref[:, h0:h0 + hd] = o_t[:, g * tq:(g + 1) * tq].T.astype(o_ref.dtype)


def _attention(cfg, u, qt, vt):
    m = u.shape[0]
    hd, kv, tq, lp = cfg.head_dim, cfg.attn_kv_heads, cfg.q_block, cfg.lp
    groups = cfg.attn_heads // kv
    hpc = min(groups, 2)
    n_q = lp // tq
    sub = cfg.attn_tk
    head = -(-cfg.n_pad // LANES) * LANES if cfg.n_pad else LANES
    n_sub = (lp - head) // sub
    assert (lp - head) % sub == 0 and n_sub % cfg.attn_unroll == 0 and sub % LANES == 0
    kern = functools.partial(_attn_kernel, groups=groups, hpc=hpc, hd=hd, head=head, sub=sub,
                             n_sub=n_sub, n_pad=cfg.n_pad, unroll=cfg.attn_unroll)
    k_col0 = cfg.q_w // hd
    return pl.pallas_call(
        kern,
        out_shape=jax.ShapeDtypeStruct((m, cfg.q_w), BF16),
        grid=(cfg.batch, kv, n_q),
        in_specs=[pl.BlockSpec((groups * hd, tq), lambda b, h, i: (h, b * n_q + i)),
                  pl.BlockSpec((lp, hd), lambda b, h, i: (b, k_col0 + h)),
                  pl.BlockSpec((hd, lp), lambda b, h, i: (h, b))],
        out_specs=pl.BlockSpec((tq, groups * hd), lambda b, h, i: (b * n_q + i, h)),
        scratch_shapes=[pltpu.VMEM((groups // hpc, hd, hpc * tq), F32)],
        compiler_params=_params("parallel", "parallel", "arbitrary"),
        name="gqa_attention",
    )(qt, u, vt)


def _chan_dft_kernel(f_ref, cs_ref, a_ref, b_ref, *, gd):
    cs = cs_ref[...]
    for g in range(f_ref.shape[1] // gd):
        ab = _dot(f_ref[:, g * gd:(g + 1) * gd], cs)
        a_ref[:, g * gd:(g + 1) * gd] = ab[:, :gd].astype(a_ref.dtype)
        b_ref[:, g * gd:(g + 1) * gd] = ab[:, gd:].astype(b_ref.dtype)


def _chan_dft(cfg, u, cs):
    m = u.shape[0]
    gd, fw = cfg.four_dim, cfg.four_w
    f0 = cfg.q_w + 2 * cfg.kv_w
    fb = math.gcd(math.gcd(f0, fw), 1024)
    tm = cfg.tm
    out = jax.ShapeDtypeStruct((m, fw), BF16)
    return pl.pallas_call(
        functools.partial(_chan_dft_kernel, gd=gd),
        out_shape=(out, out),
        grid=(m // tm, fw // fb),
        in_specs=[pl.BlockSpec((tm, fb), lambda i, j: (i, f0 // fb + j)),
                  pl.BlockSpec((gd, 2 * gd), lambda i, j: (0, 0))],
        out_specs=(pl.BlockSpec((tm, fb), lambda i, j: (i, j)),
                   pl.BlockSpec((tm, fb), lambda i, j: (i, j))),
        compiler_params=_params("parallel", "parallel"),
        name="fourier_channel_dft",
    )(u, cs)


def _seq_dft_kernel(wc_ref, ws_ref, a_ref, b_ref, o_ref, acc_ref):
    k = pl.program_id(3)

    @pl.when(k == 0)
    def _():
        acc_ref[...] = jnp.zeros_like(acc_ref)

    acc_ref[...] += _dot(wc_ref[...], a_ref[...]) - _dot(ws_ref[...], b_ref[...])

    @pl.when(k == pl.num_programs(3) - 1)
    def _():
        o_ref[...] = acc_ref[...].astype(o_ref.dtype)


def _seq_dft(cfg, wc, ws, a, b):
    m, fw = a.shape
    lp = cfg.lp
    tm, tn, tk = cfg.dft_tm, cfg.dft_tn, cfg.dft_tk
    n_m, n_k = lp // tm, lp // tk
    return pl.pallas_call(
        _seq_dft_kernel,
        out_shape=jax.ShapeDtypeStruct((m, fw), BF16),
        grid=(cfg.batch, fw // tn, n_m, n_k),
        in_specs=[pl.BlockSpec((tm, tk), lambda bb, n, i, k: (i, k)),
                  pl.BlockSpec((tm, tk), lambda bb, n, i, k: (i, k)),
                  pl.BlockSpec((tk, tn), lambda bb, n, i, k: (bb * n_k + k, n)),
                  pl.BlockSpec((tk, tn), lambda bb, n, i, k: (bb * n_k + k, n))],
        out_specs=pl.BlockSpec((tm, tn), lambda bb, n, i, k: (bb * n_m + i, n)),
        scratch_shapes=[pltpu.VMEM((tm, tn), F32)],
        compiler_params=_params("parallel", "parallel", "parallel", "arbitrary"),
        name="fourier_sequence_dft",
    )(wc, ws, a, b)


def _odd_inproj_kernel(h_ref, g_ref, w_ref, ga_ref, o_ref, low_ref, xn_ref, *, n_q_tiles,
                       q_scale, eps):
    _normalize_rows(h_ref, g_ref, xn_ref, eps)
    x = xn_ref[...]

    @pl.when(pl.program_id(1) == 0)
    def _():
        low_ref[...] = _dot(x, ga_ref[...])

    acc = _dot(x, w_ref[...])
    sc = jnp.where(pl.program_id(1) < n_q_tiles, q_scale, 1.0).astype(F32)
    o_ref[...] = (acc * sc).astype(o_ref.dtype)


def _odd_inproj(cfg, h, gain, w, layer_i, ga):
    m, d = h.shape
    n = w.shape[-1]
    r2 = ga.shape[-1]
    tm, tn = cfg.tm, cfg.tn
    assert tm % NORM_ROWS == 0
    kern = functools.partial(_odd_inproj_kernel, n_q_tiles=cfg.gla_heads * cfg.gla_dk // tn,
                             q_scale=cfg.gla_dk ** -0.5, eps=EPS)
    return pl.pallas_call(
        kern,
        out_shape=(jax.ShapeDtypeStruct((m, n), BF16), jax.ShapeDtypeStruct((m, r2), F32)),
        grid=(m // tm, n // tn),
        in_specs=_norm_specs(tm, d) + [
            pl.BlockSpec((None, d, tn), lambda i, j: (layer_i, 0, j)),
            pl.BlockSpec((None, d, r2), lambda i, j: (layer_i, 0, 0))],
        out_specs=(pl.BlockSpec((tm, tn), lambda i, j: (i, j)),
                   pl.BlockSpec((tm, r2), lambda i, j: (i, 0))),
        scratch_shapes=[pltpu.VMEM((tm, d), BF16)],
        compiler_params=_params("parallel", "arbitrary"),
        name="odd_inproj",
    )(h, gain.reshape(1, d), w, ga)


def _gla_kernel(*refs, rev, chunk, sub, halo, hpb, dk, dv, n_chunks, n_pad, tau, eps):
    if rev:
        (q_ref, k_ref, v_ref, low_ref, gb_ref, bias_ref, tri_ref, lmask_ref,
         ofw_ref, r_ref, hn_ref, o_ref, s_ref, cump_ref, kfp_ref) = refs
    else:
        (q_ref, k_ref, v_ref, low_ref, gb_ref, bias_ref, tri_ref, lmask_ref,
         o_ref, s_ref, cump_ref, kfp_ref) = refs
    c = pl.program_id(2)
    pos_chunk = (n_chunks - 1 - c) if rev else c

    @pl.when(c == 0)
    def _():
        s_ref[...] = jnp.zeros_like(s_ref)
        cump_ref[...] = jnp.zeros_like(cump_ref)
        kfp_ref[...] = jnp.zeros_like(kfp_ref)

    row = pos_chunk * chunk + lax.broadcasted_iota(jnp.int32, (chunk, 1), 0)
    valid = row >= n_pad
    col_minus_row = (lax.broadcasted_iota(jnp.int32, (chunk, chunk), 1)
                     - lax.broadcasted_iota(jnp.int32, (chunk, chunk), 0))
    row_in_sub = lax.broadcasted_iota(jnp.int32, (chunk, 1), 0) % sub
    low = low_ref[...]
    tri = tri_ref[...]

    def head(hx):
        ks = slice(hx * dk, (hx + 1) * dk)
        vs = slice(hx * dv, (hx + 1) * dv)
        z = jnp.dot(low, gb_ref[:, ks], preferred_element_type=F32,
                    precision=lax.Precision.HIGHEST) + bias_ref[:, ks]
        log_a = (jnp.minimum(z, 0.0) - jnp.log(1.0 + jnp.exp(-jnp.abs(z)))) * (1.0 / tau)
        log_a = jnp.where(valid, log_a, 0.0)
        kf = jnp.where(valid, k_ref[:, ks].astype(F32), 0.0)
        qf = q_ref[:, ks].astype(F32)
        v = v_ref[:, vs]

        g_hi = log_a.astype(BF16)
        g_lo = (log_a - g_hi.astype(F32)).astype(BF16)
        cum = _dot(tri, g_hi) + _dot(tri, g_lo)
        cum_end = cum[0:1] if rev else cum[chunk - 1:chunk]
        yield

        s_old = s_ref[hx]
        inter = _dot_nt((qf * jnp.exp(cum)).astype(BF16), s_old.astype(BF16))
        yield

        a_acc = jnp.zeros((chunk, chunk), F32)
        s_blk, level = chunk // 2, 0
        while s_blk >= sub:
            grp = 2 * s_blk
            ref_idx = s_blk if rev else s_blk - 1
            ref = jnp.broadcast_to(
                cum.reshape(chunk // grp, grp, dk)[:, ref_idx:ref_idx + 1, :],
                (chunk // grp, grp, dk)).reshape(chunk, dk)
            e = jnp.exp(-jnp.abs(cum - ref))
            a_lvl = _dot_nt((qf * e).astype(BF16), (kf * e).astype(BF16))
            a_acc = a_acc + a_lvl * lmask_ref[level]
            s_blk //= 2
            level += 1
            yield

        cump_ref[hx, halo:halo + chunk, :] = cum
        kfp_ref[hx, halo:halo + chunk, :] = kf
        a_acc = a_acc + jnp.where(col_minus_row == 0,
                                  jnp.sum(qf * kf, axis=-1, keepdims=True), 0.0)
        for delta in range(1, sub):
            off = halo + delta if rev else halo - delta
            k_sh = kfp_ref[hx, off:off + chunk, :]
            cum_sh = cump_ref[hx, off:off + chunk, :]
            col = jnp.sum(qf * k_sh * jnp.exp(cum - cum_sh), axis=-1, keepdims=True)
            same_sub = (row_in_sub + delta < sub) if rev else (row_in_sub >= delta)
            col = jnp.where(same_sub, col, 0.0)
            a_acc = a_acc + jnp.where(col_minus_row == (delta if rev else -delta), col, 0.0)
            yield

        o = inter + _dot(a_acc.astype(BF16), v)

        k_out = (kf * jnp.exp(cum_end - cum)).astype(BF16)
        s_ref[hx] = s_old * jnp.exp(cum_end) + _dot_tn(v, k_out)
        yield

        if rev:
            tot = ofw_ref[:, vs] + o
            ms = jnp.mean(tot * tot, axis=-1, keepdims=True)
            y = tot * lax.rsqrt(ms + eps) * hn_ref[...]
            r = r_ref[:, vs].astype(F32)
            o_ref[:, vs] = (y * r * jax.nn.sigmoid(r)).astype(o_ref.dtype)
        else:
            o_ref[:, vs] = o

    chains = [head(hx) for hx in range(hpb)]
    while chains:
        for g in list(chains):
            if next(g, "done") == "done":
                chains.remove(g)


def _gla_constants(cfg, rev):
    c, sub = cfg.gla_chunk, cfg.gla_sub
    i = jnp.arange(c)[:, None]
    j = jnp.arange(c)[None, :]
    tri = ((j >= i) if rev else (j <= i)).astype(BF16)
    masks = []
    s = c // 2
    while s >= sub:
        i_hi, j_hi = (i // s) % 2 == 1, (j // s) % 2 == 1
        roles = (~i_hi & j_hi) if rev else (i_hi & ~j_hi)
        masks.append(((i // (2 * s) == j // (2 * s)) & roles).astype(F32))
        s //= 2
    return tri, jnp.stack(masks)


def _gla(cfg, u, low, gate_b, gate_bias, rev, o_fw=None, head_norm=None):
    m = u.shape[0]
    hh, dk, dv, c = cfg.gla_heads, cfg.gla_dk, cfg.gla_dv, cfg.gla_chunk
    rank = cfg.gla_rank
    n_chunks = cfg.lp // c
    tri, lmask = _gla_constants(cfg, rev)
    n_lvl = lmask.shape[0]
    hpb = cfg.gla_heads_per_step
    kw, vw = hpb * dk, hpb * dv
    assert hh % hpb == 0 and (2 * hh * dk) % vw == 0
    k_col0 = hh // hpb
    v_col0 = 2 * hh * dk // vw
    r_col0 = v_col0 + hh // hpb

    def rowblk(b, h, ci):
        return b * n_chunks + ((n_chunks - 1 - ci) if rev else ci)

    in_specs = [
        pl.BlockSpec((c, kw), lambda b, h, ci: (rowblk(b, h, ci), h)),
        pl.BlockSpec((c, kw), lambda b, h, ci: (rowblk(b, h, ci), k_col0 + h)),
        pl.BlockSpec((c, vw), lambda b, h, ci: (rowblk(b, h, ci), v_col0 + h)),
        pl.BlockSpec((c, rank), lambda b, h, ci: (rowblk(b, h, ci), 0)),
        pl.BlockSpec((rank, kw), lambda b, h, ci: (0, h)),
        pl.BlockSpec((1, kw), lambda b, h, ci: (0, h)),
        pl.BlockSpec((c, c), lambda b, h, ci: (0, 0)),
        pl.BlockSpec((n_lvl, c, c), lambda b, h, ci: (0, 0, 0)),
    ]
    args = [u, u, u, low, gate_b, gate_bias.reshape(1, -1), tri, lmask]
    if rev:
        in_specs += [
            pl.BlockSpec((c, vw), lambda b, h, ci: (rowblk(b, h, ci), h)),
            pl.BlockSpec((c, vw), lambda b, h, ci: (rowblk(b, h, ci), r_col0 + h)),
            pl.BlockSpec((1, dv), lambda b, h, ci: (0, 0)),
        ]
        args += [o_fw, u, head_norm.reshape(1, dv)]
    halo = -(-cfg.gla_sub // SUBLANES) * SUBLANES
    kern = functools.partial(_gla_kernel, rev=rev, chunk=c, sub=cfg.gla_sub, halo=halo,
                             hpb=hpb, dk=dk, dv=dv, n_chunks=n_chunks, n_pad=cfg.n_pad,
                             tau=GLA_GATE_TAU, eps=EPS)
    return pl.pallas_call(
        kern,
        out_shape=jax.ShapeDtypeStruct((m, hh * dv), BF16 if rev else F32),
        grid=(cfg.batch, hh // hpb, n_chunks),
        in_specs=in_specs,
        out_specs=pl.BlockSpec((c, vw), lambda b, h, ci: (rowblk(b, h, ci), h)),
        scratch_shapes=[pltpu.VMEM((hpb, dv, dk), F32),
                        pltpu.VMEM((hpb, c + 2 * halo, dk), F32),
                        pltpu.VMEM((hpb, c + 2 * halo, dk), F32)],
        compiler_params=_params("parallel", "parallel", "arbitrary"),
        name="gla_reverse" if rev else "gla_forward",
    )(*args)


def _rope_tables(cfg):
    hd = cfg.head_dim
    n_freq = hd // 4
    t = jnp.arange(cfg.seq)
    rows = (t // cfg.grid_w).astype(F32)
    cols = (t % cfg.grid_w).astype(F32)
    inv_freq = jnp.power(ROPE_THETA, -jnp.arange(n_freq, dtype=F32) / n_freq)
    ang = jnp.concatenate([jnp.tile(rows[:, None] * inv_freq, (1, 2)),
                           jnp.tile(cols[:, None] * inv_freq, (1, 2))], axis=1)
    ang = jnp.concatenate([jnp.zeros((cfg.n_pad + cfg.n_meta, hd), F32), ang], axis=0)
    sign = jnp.where((jnp.arange(hd) % (hd // 2)) < n_freq, -1.0, 1.0).astype(F32)
    cos_t = jnp.tile(jnp.cos(ang), (cfg.batch, 1))
    sin_t = jnp.tile(jnp.sin(ang) * sign, (cfg.batch, 1))
    return cos_t, sin_t


def _dft_tables(cfg):
    ll, gd, lp, n_pad = cfg.length, cfg.four_dim, cfg.lp, cfg.n_pad
    scale = 1.0 / math.sqrt(ll * gd)
    n_hi = -(-ll // LANES)
    t = jnp.arange(ll, dtype=jnp.int32)[None, :]
    k_lo = jnp.arange(LANES, dtype=jnp.int32)[:, None]
    k_hi = jnp.arange(n_hi, dtype=jnp.int32)[:, None] * LANES
    ang_lo = (2.0 * math.pi / ll) * ((k_lo * t) % ll).astype(F32)
    ang_hi = (2.0 * math.pi / ll) * ((k_hi * t) % ll).astype(F32)
    c_lo, s_lo = jnp.cos(ang_lo)[None], jnp.sin(ang_lo)[None]
    c_hi, s_hi = (jnp.cos(ang_hi) * scale)[:, None], (jnp.sin(ang_hi) * scale)[:, None]
    wc = (c_hi * c_lo - s_hi * s_lo).reshape(n_hi * LANES, ll)[:ll]
    ws = (s_hi * c_lo + c_hi * s_lo).reshape(n_hi * LANES, ll)[:ll]
    pad = ((n_pad, lp - n_pad - ll), (n_pad, lp - n_pad - ll))
    wc = jnp.pad(wc.astype(BF16), pad)
    ws = jnp.pad(ws.astype(BF16), pad)
    cc = jnp.arange(gd, dtype=jnp.int32)
    ang_c = (2.0 * math.pi / gd) * ((cc[:, None] * cc[None, :]) % gd).astype(F32)
    cs = jnp.concatenate([jnp.cos(ang_c), jnp.sin(ang_c)], axis=1).astype(BF16)
    return wc, ws, cs


def _forward(cfg, x, meta_tokens, pre_norm, ffn_w_gate, ffn_w_up, ffn_w_down, even_w_in,
             even_q_norm, even_k_norm, even_w_out, odd_w_in, odd_gate_a, odd_gate_b,
             odd_gate_bias, odd_head_norm, odd_w_out):
    bsz, d = cfg.batch, cfg.d_model
    meta = jnp.broadcast_to(meta_tokens.astype(F32)[None], (bsz, cfg.n_meta, d))
    h = jnp.concatenate([jnp.zeros((bsz, cfg.n_pad, d), F32), meta, x.astype(F32)], axis=1)
    h = h.reshape(cfg.rows, d)

    wg, wu, wd = ffn_w_gate.astype(BF16), ffn_w_up.astype(BF16), ffn_w_down.astype(BF16)
    ew_in, ew_out = even_w_in.astype(BF16), even_w_out.astype(BF16)
    ow_in, ow_out = odd_w_in.astype(BF16), odd_w_out.astype(BF16)
    ga = jnp.concatenate([odd_gate_a[:, 0], odd_gate_a[:, 1]], axis=-1).astype(BF16)

    cos_t, sin_t = _rope_tables(cfg)
    wc, ws, cs = _dft_tables(cfg)

    def ffn(h, layer, slot, norm_slot):
        a = _ffn_up(cfg, h, pre_norm[layer, norm_slot], wg, wu, layer, slot)
        return _mm_res(cfg, [a], wd, (layer, slot), h, 0.5)

    for layer in range(cfg.depth):
        i = layer // 2
        h = ffn(h, layer, 0, 0)
        mix_gain = pre_norm[layer, 1]
        if layer % 2 == 0:
            u, qt, vt = _even_inproj(cfg, h, mix_gain, ew_in, i, cos_t, sin_t, even_q_norm[i],
                                     even_k_norm[i])
            attn = _attention(cfg, u, qt, vt)
            fa, fb = _chan_dft(cfg, u, cs)
            four = _seq_dft(cfg, wc, ws, fa, fb)
            h = _mm_res(cfg, [attn, four], ew_out, (i,), h, 1.0)
        else:
            u, low = _odd_inproj(cfg, h, mix_gain, ow_in, i, ga)
            rank = cfg.gla_rank
            o_fw = _gla(cfg, u, low[:, :rank], odd_gate_b[i, 0], odd_gate_bias[i, 0], rev=False)
            og = _gla(cfg, u, low[:, rank:], odd_gate_b[i, 1], odd_gate_bias[i, 1], rev=True,
                      o_fw=o_fw, head_norm=odd_head_norm[i])
            h = _mm_res(cfg, [og], ow_out, (i,), h, 1.0)
        h = ffn(h, layer, 1, 2)
    return h.reshape(bsz, cfg.lp, d)[:, cfg.n_pad + cfg.n_meta:]


def kernel(x, meta_tokens, pre_norm, ffn_w_gate, ffn_w_up, ffn_w_down, even_w_in, even_q_norm,
           even_k_norm, even_w_out, odd_w_in, odd_gate_a, odd_gate_b, odd_gate_bias,
           odd_head_norm, odd_w_out):
    bsz, seq, d = x.shape
    cfg = Cfg(batch=bsz, seq=seq, d_model=d, d_ff=ffn_w_gate.shape[-1], depth=pre_norm.shape[0],
              n_meta=N_META, grid_w=GRID_W, attn_heads=ATTN_HEADS, attn_kv_heads=ATTN_KV_HEADS,
              head_dim=HEAD_DIM, q_block=Q_BLOCK, four_groups=FOURIER_GROUPS,
              four_dim=FOURIER_GROUP_DIM, gla_heads=GLA_HEADS, gla_rank=odd_gate_a.shape[-1],
              tm_res=1280, tm=640, tn=512, attn_tk=1024, attn_unroll=8, dft_tm=1664, dft_tn=1024, dft_tk=640,
              gla_chunk=128, gla_sub=4, gla_heads_per_step=4)
    return _forward(cfg, x, meta_tokens, pre_norm, ffn_w_gate, ffn_w_up, ffn_w_down, even_w_in,
                    even_q_norm, even_k_norm, even_w_out, odd_w_in, odd_gate_a, odd_gate_b,
                    odd_gate_bias, odd_head_norm, odd_w_out)
```

```python
import functools
import math
from typing import NamedTuple

import jax
import jax.numpy as jnp
from jax import lax
from jax.experimental import pallas as pl
from jax.experimental.pallas import tpu as pltpu

F32 = jnp.float32
BF16 = jnp.bfloat16

N_META = 16
GRID_W = 64
EPS = 1e-6
ATTN_HEADS = 16
ATTN_KV_HEADS = 4
HEAD_DIM = 128
Q_BLOCK = 128
ROPE_THETA = 10000.0
FOURIER_GROUPS = 16
FOURIER_GROUP_DIM = 128
GLA_HEADS = 8
GLA_GATE_TAU = 16.0

V7X_VMEM_BYTES = 64 * 1024 * 1024
VMEM_LIMIT_BYTES = V7X_VMEM_BYTES - 8 * 1024 * 1024
LANES = 128
SUBLANES = 8
NEG_BIG = -1e30


class Cfg(NamedTuple):
    batch: int
    seq: int
    d_model: int
    d_ff: int
    depth: int
    n_meta: int
    grid_w: int
    attn_heads: int
    attn_kv_heads: int
    head_dim: int
    q_block: int
    four_groups: int
    four_dim: int
    gla_heads: int
    gla_rank: int
    tm_res: int
    tm: int
    tn: int
    attn_tk: int
    attn_unroll: int
    dft_tm: int
    dft_tn: int
    dft_tk: int
    gla_chunk: int
    gla_sub: int
    gla_heads_per_step: int

    @property
    def length(self):
        return self.n_meta + self.seq

    @property
    def n_pad(self):
        return (-self.length) % self.q_block

    @property
    def lp(self):
        return self.length + self.n_pad

    @property
    def rows(self):
        return self.batch * self.lp

    @property
    def q_w(self):
        return self.attn_heads * self.head_dim

    @property
    def kv_w(self):
        return self.attn_kv_heads * self.head_dim

    @property
    def four_w(self):
        return self.four_groups * self.four_dim

    @property
    def gla_dk(self):
        return self.d_model // (2 * self.gla_heads)

    @property
    def gla_dv(self):
        return self.d_model // self.gla_heads


def _params(*sem):
    return pltpu.CompilerParams(dimension_semantics=sem, vmem_limit_bytes=VMEM_LIMIT_BYTES)


def _dot(a, b):
    return jnp.dot(a, b, preferred_element_type=F32)


def _dot_nt(a, b):
    return lax.dot_general(a, b, (((1,), (1,)), ((), ())), preferred_element_type=F32)


def _dot_tn(a, b):
    return lax.dot_general(a, b, (((0,), (0,)), ((), ())), preferred_element_type=F32)


NORM_ROWS = 16
NORM_UNROLL = 8


def _normalize_rows(h_ref, g_ref, xn_ref, eps):
    @pl.when(pl.program_id(1) == 0)
    def _():
        gain = g_ref[...]

        def body(r, carry):
            rows = pl.ds(pl.multiple_of(r * NORM_ROWS, NORM_ROWS), NORM_ROWS)
            x = h_ref[rows, :]
            ms = jnp.mean(x * x, axis=-1, keepdims=True)
            xn_ref[rows, :] = (x * lax.rsqrt(ms + eps) * gain).astype(xn_ref.dtype)
            return carry

        lax.fori_loop(0, h_ref.shape[0] // NORM_ROWS, body, 0, unroll=NORM_UNROLL)


def _norm_specs(tm, d):
    return [pl.BlockSpec((tm, d), lambda i, j: (i, 0)), pl.BlockSpec((1, d), lambda i, j: (0, 0))]


def _ffn_up_kernel(h_ref, g_ref, wg_ref, wu_ref, o_ref, xn_ref, *, eps):
    _normalize_rows(h_ref, g_ref, xn_ref, eps)
    x = xn_ref[...]
    g = _dot(x, wg_ref[...])
    u = _dot(x, wu_ref[...])
    o_ref[...] = (g * jax.nn.sigmoid(g) * u).astype(o_ref.dtype)


def _ffn_up(cfg, h, gain, wg, wu, layer, slot):
    m, d = h.shape
    f = wg.shape[-1]
    tm, tn = cfg.tm, cfg.tn
    assert tm % NORM_ROWS == 0
    wspec = pl.BlockSpec((None, None, d, tn), lambda i, j: (layer, slot, 0, j))
    return pl.pallas_call(
        functools.partial(_ffn_up_kernel, eps=EPS),
        out_shape=jax.ShapeDtypeStruct((m, f), BF16),
        grid=(m // tm, f // tn),
        in_specs=_norm_specs(tm, d) + [wspec, wspec],
        out_specs=pl.BlockSpec((tm, tn), lambda i, j: (i, j)),
        scratch_shapes=[pltpu.VMEM((tm, d), BF16)],
        compiler_params=_params("parallel", "arbitrary"),
        name="ffn_up",
    )(h, gain.reshape(1, d), wg, wu)


def _mm_res_kernel(*refs, n_lhs, scale):
    xs, ws = refs[:n_lhs], refs[n_lhs:2 * n_lhs]
    h_ref, o_ref = refs[2 * n_lhs], refs[2 * n_lhs + 1]
    acc = _dot(xs[0][...], ws[0][...])
    for x_ref, w_ref in zip(xs[1:], ws[1:]):
        acc = acc + _dot(x_ref[...], w_ref[...])
    o_ref[...] = h_ref[...] + scale * acc


def _mm_res(cfg, xs, w, w_lead, h, scale):
    m, d = h.shape
    kx = xs[0].shape[1]
    n_lhs = len(xs)
    tm, tn = cfg.tm_res, cfg.tn
    n_lead = len(w_lead)
    w_specs = [
        pl.BlockSpec((None,) * n_lead + (kx, tn), lambda i, j, p=p: tuple(w_lead) + (p, j))
        for p in range(n_lhs)
    ]
    return pl.pallas_call(
        functools.partial(_mm_res_kernel, n_lhs=n_lhs, scale=scale),
        out_shape=jax.ShapeDtypeStruct((m, d), F32),
        grid=(m // tm, d // tn),
        in_specs=[pl.BlockSpec((tm, kx), lambda i, j: (i, 0))] * n_lhs + w_specs
        + [pl.BlockSpec((tm, tn), lambda i, j: (i, j))],
        out_specs=pl.BlockSpec((tm, tn), lambda i, j: (i, j)),
        input_output_aliases={2 * n_lhs: 0},
        compiler_params=_params("parallel", "arbitrary"),
        name="mm_residual",
    )(*xs, *([w] * n_lhs), h)


def _even_inproj_kernel(h_ref, g_ref, w_ref, cos_ref, sin_ref, qg_ref, kg_ref, o_ref, qt_ref,
                        vt_ref, xn_ref, *, n_q_tiles, n_k_tiles, hd, eps, q_scale):
    _normalize_rows(h_ref, g_ref, xn_ref, eps)
    j = pl.program_id(1)
    acc = _dot(xn_ref[...], w_ref[...])
    heads_per_tile = acc.shape[1] // hd

    def normed_rotated(gain):
        c = cos_ref[...]
        s = sin_ref[...]
        lane = lax.broadcasted_iota(jnp.int32, c.shape, 1)
        first_half = (lane % (hd // 2)) < (hd // 4)
        for hh in range(heads_per_tile):
            xh = acc[:, hh * hd:(hh + 1) * hd]
            ms = jnp.mean(xh * xh, axis=-1, keepdims=True)
            y = xh * lax.rsqrt(ms + eps) * gain
            partner = jnp.where(first_half,
                                pltpu.roll(y, hd - hd // 4, axis=1),
                                pltpu.roll(y, hd // 4, axis=1))
            yield hh, y * c + partner * s

    @pl.when(j < n_q_tiles)
    def _():
        for hh, y in normed_rotated(qg_ref[...] * q_scale):
            o_ref[:, hh * hd:(hh + 1) * hd] = y.astype(o_ref.dtype)
            qt_ref[hh * hd:(hh + 1) * hd, :] = y.T.astype(qt_ref.dtype)

    @pl.when((j >= n_q_tiles) & (j < n_q_tiles + n_k_tiles))
    def _():
        for hh, y in normed_rotated(kg_ref[...]):
            o_ref[:, hh * hd:(hh + 1) * hd] = y.astype(o_ref.dtype)

    @pl.when(j >= n_q_tiles + n_k_tiles)
    def _():
        o_ref[...] = acc.astype(o_ref.dtype)

    @pl.when((j >= n_q_tiles + n_k_tiles) & (j < n_q_tiles + 2 * n_k_tiles))
    def _():
        vt_ref[...] = acc.T.astype(vt_ref.dtype)


def _even_inproj(cfg, h, gain, w, layer_i, cos_t, sin_t, q_gain, k_gain):
    m, d = h.shape
    n = w.shape[-1]
    tm, tn, hd = cfg.tm, cfg.tn, cfg.head_dim
    assert cfg.q_w % tn == 0 and cfg.kv_w % tn == 0 and tn % hd == 0 and tm % NORM_ROWS == 0
    n_q, n_k = cfg.q_w // tn, cfg.kv_w // tn
    kern = functools.partial(
        _even_inproj_kernel, n_q_tiles=n_q, n_k_tiles=n_k, hd=hd,
        eps=EPS, q_scale=hd ** -0.5 * math.log2(math.e))
    return pl.pallas_call(
        kern,
        out_shape=(jax.ShapeDtypeStruct((m, n), BF16),
                   jax.ShapeDtypeStruct((cfg.q_w, m), BF16),
                   jax.ShapeDtypeStruct((cfg.kv_w, m), BF16)),
        grid=(m // tm, n // tn),
        in_specs=_norm_specs(tm, d) + [
            pl.BlockSpec((None, d, tn), lambda i, j: (layer_i, 0, j)),
            pl.BlockSpec((tm, hd), lambda i, j: (i, 0)),
            pl.BlockSpec((tm, hd), lambda i, j: (i, 0)),
            pl.BlockSpec((1, hd), lambda i, j: (0, 0)),
            pl.BlockSpec((1, hd), lambda i, j: (0, 0))],
        out_specs=(
            pl.BlockSpec((tm, tn), lambda i, j: (i, j)),
            pl.BlockSpec((tn, tm), lambda i, j: (jnp.minimum(j, n_q - 1), i)),
            pl.BlockSpec((tn, tm), lambda i, j: (jnp.clip(j - n_q - n_k, 0, n_k - 1), i))),
        scratch_shapes=[pltpu.VMEM((tm, d), BF16)],
        compiler_params=_params("parallel", "arbitrary"),
        name="even_inproj",
    )(h, gain.reshape(1, d), w, cos_t, sin_t, q_gain.reshape(1, hd), k_gain.reshape(1, hd))


def _attn_kernel(qt_ref, k_ref, vt_ref, o_ref, acc_ref, *, groups, hpc, hd, head, sub, n_sub,
                 n_pad, unroll):
    tq = qt_ref.shape[1]
    n_chain = groups // hpc
    cw = hpc * tq
    q_t = [jnp.concatenate([qt_ref[(c * hpc + g) * hd:(c * hpc + g + 1) * hd, :]
                            for g in range(hpc)], axis=1) for c in range(n_chain)]

    def scores(start, size):
        k_blk = k_ref[pl.ds(start, size), :]
        return tuple(_dot(k_blk, q_t[c]) for c in range(n_chain))

    def consume(start, size, st, ml, masked):
        v_blk = vt_ref[:, pl.ds(start, size)]
        out = []
        for c in range(n_chain):
            m_prev, l_prev = ml[c]
            s = st[c]
            if masked:
                row = lax.broadcasted_iota(jnp.int32, s.shape, 0)
                s = jnp.where(row < n_pad, NEG_BIG, s)
            m_new = jnp.maximum(m_prev, jnp.max(s, axis=0, keepdims=True))
            alpha = jnp.exp2(m_prev - m_new)
            p = jnp.exp2(s - m_new)
            l_new = alpha * l_prev + jnp.sum(p, axis=0, keepdims=True)
            acc_ref[c] = alpha * acc_ref[c] + _dot(v_blk, p.astype(v_blk.dtype))
            out.append((m_new, l_new))
        return tuple(out)

    acc_ref[...] = jnp.zeros_like(acc_ref)
    ml = tuple((jnp.full((1, cw), NEG_BIG, F32), jnp.zeros((1, cw), F32))
               for _ in range(n_chain))
    def sub_start(j):
        return pl.multiple_of(head + j * sub, LANES)

    st = scores(0, head)
    st_next = scores(sub_start(0), sub)
    ml = consume(0, head, st, ml, True)

    def body(it, carry):
        st, ml = carry
        for u in range(unroll):
            j = it * unroll + u
            st_next = scores(sub_start(jnp.minimum(j + 1, n_sub - 1)), sub)
            ml = consume(sub_start(j), sub, st, ml, False)
            st = st_next
        return st, ml

    _, ml = lax.fori_loop(0, n_sub // unroll, body, (st_next, ml))
    for c in range(n_chain):
        o_t = acc_ref[c] / ml[c][1]
        for g in range(hpc):
            h0 = (c * hpc + g) * hd
            o_ref[:, h0:h0 + hd] = o_t[:, g * tq:(g + 1) * tq].T.astype(o_ref.dtype)


def _attention(cfg, u, qt, vt):
    m = u.shape[0]
    hd, kv, tq, lp = cfg.head_dim, cfg.attn_kv_heads, cfg.q_block, cfg.lp
    groups = cfg.attn_heads // kv
    hpc = min(groups, 2)
    n_q = lp // tq
    sub = cfg.attn_tk
    n_sub = (lp - cfg.n_pad) // sub - 1
    head = lp - n_sub * sub
    assert head % LANES == 0 and cfg.n_pad <= head and n_sub >= 1
    assert n_sub % cfg.attn_unroll == 0 and sub % LANES == 0
    kern = functools.partial(_attn_kernel, groups=groups, hpc=hpc, hd=hd, head=head, sub=sub,
                             n_sub=n_sub, n_pad=cfg.n_pad, unroll=cfg.attn_unroll)
    k_col0 = cfg.q_w // hd
    return pl.pallas_call(
        kern,
        out_shape=jax.ShapeDtypeStruct((m, cfg.q_w), BF16),
        grid=(cfg.batch, kv, n_q),
        in_specs=[pl.BlockSpec((groups * hd, tq), lambda b, h, i: (h, b * n_q + i)),
                  pl.BlockSpec((lp, hd), lambda b, h, i: (b, k_col0 + h)),
                  pl.BlockSpec((hd, lp), lambda b, h, i: (h, b))],
        out_specs=pl.BlockSpec((tq, groups * hd), lambda b, h, i: (b * n_q + i, h)),
        scratch_shapes=[pltpu.VMEM((groups // hpc, hd, hpc * tq), F32)],
        compiler_params=_params("parallel", "parallel", "arbitrary"),
        name="gqa_attention",
    )(qt, u, vt)


def _chan_dft_kernel(f_ref, cs_ref, a_ref, b_ref, *, gd):
    cs = cs_ref[...]
    for g in range(f_ref.shape[1] // gd):
        ab = _dot(f_ref[:, g * gd:(g + 1) * gd], cs)
        a_ref[:, g * gd:(g + 1) * gd] = ab[:, :gd].astype(a_ref.dtype)
        b_ref[:, g * gd:(g + 1) * gd] = ab[:, gd:].astype(b_ref.dtype)


def _chan_dft(cfg, u, cs):
    m = u.shape[0]
    gd, fw = cfg.four_dim, cfg.four_w
    f0 = cfg.q_w + 2 * cfg.kv_w
    fb = math.gcd(math.gcd(f0, fw), 1024)
    tm = cfg.tm
    out = jax.ShapeDtypeStruct((m, fw), BF16)
    return pl.pallas_call(
        functools.partial(_chan_dft_kernel, gd=gd),
        out_shape=(out, out),
        grid=(m // tm, fw // fb),
        in_specs=[pl.BlockSpec((tm, fb), lambda i, j: (i, f0 // fb + j)),
                  pl.BlockSpec((gd, 2 * gd), lambda i, j: (0, 0))],
        out_specs=(pl.BlockSpec((tm, fb), lambda i, j: (i, j)),
                   pl.BlockSpec((tm, fb), lambda i, j: (i, j))),
        compiler_params=_params("parallel", "parallel"),
        name="fourier_channel_dft",
    )(u, cs)


def _seq_dft_kernel(wc_ref, ws_ref, a_ref, b_ref, o_ref, acc_ref):
    k = pl.program_id(3)

    @pl.when(k == 0)
    def _():
        acc_ref[...] = jnp.zeros_like(acc_ref)

    acc_ref[...] += _dot(wc_ref[...], a_ref[...]) - _dot(ws_ref[...], b_ref[...])

    @pl.when(k == pl.num_programs(3) - 1)
    def _():
        o_ref[...] = acc_ref[...].astype(o_ref.dtype)


def _seq_dft(cfg, wc, ws, a, b):
    m, fw = a.shape
    lp = cfg.lp
    tm, tn, tk = cfg.dft_tm, cfg.dft_tn, cfg.dft_tk
    n_m, n_k = lp // tm, lp // tk
    return pl.pallas_call(
        _seq_dft_kernel,
        out_shape=jax.ShapeDtypeStruct((m, fw), BF16),
        grid=(cfg.batch, fw // tn, n_m, n_k),
        in_specs=[pl.BlockSpec((tm, tk), lambda bb, n, i, k: (i, k)),
                  pl.BlockSpec((tm, tk), lambda bb, n, i, k: (i, k)),
                  pl.BlockSpec((tk, tn), lambda bb, n, i, k: (bb * n_k + k, n)),
                  pl.BlockSpec((tk, tn), lambda bb, n, i, k: (bb * n_k + k, n))],
        out_specs=pl.BlockSpec((tm, tn), lambda bb, n, i, k: (bb * n_m + i, n)),
        scratch_shapes=[pltpu.VMEM((tm, tn), F32)],
        compiler_params=_params("parallel", "parallel", "parallel", "arbitrary"),
        name="fourier_sequence_dft",
    )(wc, ws, a, b)


def _odd_inproj_kernel(h_ref, g_ref, w_ref, ga_ref, o_ref, low_ref, xn_ref, *, n_q_tiles,
                       q_scale, eps):
    _normalize_rows(h_ref, g_ref, xn_ref, eps)

    @pl.when(pl.program_id(1) == 0)
    def _():
        low_ref[...] = _dot(xn_ref[...], ga_ref[...])

    acc = _dot(xn_ref[...], w_ref[...])
    sc = jnp.where(pl.program_id(1) < n_q_tiles, q_scale, 1.0).astype(F32)
    o_ref[...] = (acc * sc).astype(o_ref.dtype)


def _odd_inproj(cfg, h, gain, w, layer_i, ga):
    m, d = h.shape
    n = w.shape[-1]
    r2 = ga.shape[-1]
    tm, tn = cfg.tm, cfg.tn
    assert tm % NORM_ROWS == 0
    kern = functools.partial(_odd_inproj_kernel, n_q_tiles=cfg.gla_heads * cfg.gla_dk // tn,
                             q_scale=cfg.gla_dk ** -0.5, eps=EPS)
    return pl.pallas_call(
        kern,
        out_shape=(jax.ShapeDtypeStruct((m, n), BF16), jax.ShapeDtypeStruct((m, r2), F32)),
        grid=(m // tm, n // tn),
        in_specs=_norm_specs(tm, d) + [
            pl.BlockSpec((None, d, tn), lambda i, j: (layer_i, 0, j)),
            pl.BlockSpec((None, d, r2), lambda i, j: (layer_i, 0, 0))],
        out_specs=(pl.BlockSpec((tm, tn), lambda i, j: (i, j)),
                   pl.BlockSpec((tm, r2), lambda i, j: (i, 0))),
        scratch_shapes=[pltpu.VMEM((tm, d), BF16)],
        compiler_params=_params("parallel", "arbitrary"),
        name="odd_inproj",
    )(h, gain.reshape(1, d), w, ga)


def _gla_kernel(*refs, rev, chunk, sub, halo, hpb, dk, dv, n_chunks, n_pad, tau, eps):
    if rev:
        (q_ref, k_ref, v_ref, low_ref, gb_ref, bias_ref, tri_ref, lmask_ref,
         ofw_ref, r_ref, hn_ref, o_ref, s_ref, cump_ref, kfp_ref) = refs
    else:
        (q_ref, k_ref, v_ref, low_ref, gb_ref, bias_ref, tri_ref, lmask_ref,
         o_ref, s_ref, cump_ref, kfp_ref) = refs
    c = pl.program_id(2)
    pos_chunk = (n_chunks - 1 - c) if rev else c

    @pl.when(c == 0)
    def _():
        s_ref[...] = jnp.zeros_like(s_ref)
        cump_ref[...] = jnp.zeros_like(cump_ref)
        kfp_ref[...] = jnp.zeros_like(kfp_ref)

    row = pos_chunk * chunk + lax.broadcasted_iota(jnp.int32, (chunk, 1), 0)
    valid = row >= n_pad
    col_minus_row = (lax.broadcasted_iota(jnp.int32, (chunk, chunk), 1)
                     - lax.broadcasted_iota(jnp.int32, (chunk, chunk), 0))
    row_in_sub = lax.broadcasted_iota(jnp.int32, (chunk, 1), 0) % sub
    low = low_ref[...]
    tri = tri_ref[...]

    def head(hx):
        ks = slice(hx * dk, (hx + 1) * dk)
        vs = slice(hx * dv, (hx + 1) * dv)
        z = jnp.dot(low, gb_ref[:, ks], preferred_element_type=F32,
                    precision=lax.Precision.HIGHEST) + bias_ref[:, ks]
        log_a = (jnp.minimum(z, 0.0) - jnp.log(1.0 + jnp.exp(-jnp.abs(z)))) * (1.0 / tau)
        log_a = jnp.where(valid, log_a, 0.0)
        kf = jnp.where(valid, k_ref[:, ks].astype(F32), 0.0)
        qf = q_ref[:, ks].astype(F32)
        v = v_ref[:, vs]

        g_hi = log_a.astype(BF16)
        g_lo = (log_a - g_hi.astype(F32)).astype(BF16)
        cum = _dot(tri, g_hi) + _dot(tri, g_lo)
        cum_end = cum[0:1] if rev else cum[chunk - 1:chunk]
        yield

        s_old = s_ref[hx]
        inter = _dot_nt((qf * jnp.exp(cum)).astype(BF16), s_old.astype(BF16))
        yield

        a_acc = jnp.zeros((chunk, chunk), F32)
        s_blk, level = chunk // 2, 0
        while s_blk >= sub:
            grp = 2 * s_blk
            ref_idx = s_blk if rev else s_blk - 1
            ref = jnp.broadcast_to(
                cum.reshape(chunk // grp, grp, dk)[:, ref_idx:ref_idx + 1, :],
                (chunk // grp, grp, dk)).reshape(chunk, dk)
            e = jnp.exp(-jnp.abs(cum - ref))
            a_lvl = _dot_nt((qf * e).astype(BF16), (kf * e).astype(BF16))
            a_acc = a_acc + a_lvl * lmask_ref[level]
            s_blk //= 2
            level += 1
            yield

        cump_ref[hx, halo:halo + chunk, :] = cum
        kfp_ref[hx, halo:halo + chunk, :] = kf
        a_acc = a_acc + jnp.where(col_minus_row == 0,
                                  jnp.sum(qf * kf, axis=-1, keepdims=True), 0.0)
        for delta in range(1, sub):
            off = halo + delta if rev else halo - delta
            k_sh = kfp_ref[hx, off:off + chunk, :]
            cum_sh = cump_ref[hx, off:off + chunk, :]
            col = jnp.sum(qf * k_sh * jnp.exp(cum - cum_sh), axis=-1, keepdims=True)
            same_sub = (row_in_sub + delta < sub) if rev else (row_in_sub >= delta)
            col = jnp.where(same_sub, col, 0.0)
            a_acc = a_acc + jnp.where(col_minus_row == (delta if rev else -delta), col, 0.0)
            yield

        o = inter + _dot(a_acc.astype(BF16), v)

        k_out = (kf * jnp.exp(cum_end - cum)).astype(BF16)
        s_ref[hx] = s_old * jnp.exp(cum_end) + _dot_tn(v, k_out)
        yield

        if rev:
            tot = ofw_ref[:, vs] + o
            ms = jnp.mean(tot * tot, axis=-1, keepdims=True)
            y = tot * lax.rsqrt(ms + eps) * hn_ref[...]
            r = r_ref[:, vs].astype(F32)
            o_ref[:, vs] = (y * r * jax.nn.sigmoid(r)).astype(o_ref.dtype)
        else:
            o_ref[:, vs] = o

    chains = [head(hx) for hx in range(hpb)]
    while chains:
        for g in list(chains):
            if next(g, "done") == "done":
                chains.remove(g)


def _gla_constants(cfg, rev):
    c, sub = cfg.gla_chunk, cfg.gla_sub
    i = jnp.arange(c)[:, None]
    j = jnp.arange(c)[None, :]
    tri = ((j >= i) if rev else (j <= i)).astype(BF16)
    masks = []
    s = c // 2
    while s >= sub:
        i_hi, j_hi = (i // s) % 2 == 1, (j // s) % 2 == 1
        roles = (~i_hi & j_hi) if rev else (i_hi & ~j_hi)
        masks.append(((i // (2 * s) == j // (2 * s)) & roles).astype(F32))
        s //= 2
    return tri, jnp.stack(masks)


def _gla(cfg, u, low, gate_b, gate_bias, rev, o_fw=None, head_norm=None):
    m = u.shape[0]
    hh, dk, dv, c = cfg.gla_heads, cfg.gla_dk, cfg.gla_dv, cfg.gla_chunk
    rank = cfg.gla_rank
    n_chunks = cfg.lp // c
    tri, lmask = _gla_constants(cfg, rev)
    n_lvl = lmask.shape[0]
    hpb = cfg.gla_heads_per_step
    kw, vw = hpb * dk, hpb * dv
    assert hh % hpb == 0 and (2 * hh * dk) % vw == 0
    k_col0 = hh // hpb
    v_col0 = 2 * hh * dk // vw
    r_col0 = v_col0 + hh // hpb

    def rowblk(b, h, ci):
        return b * n_chunks + ((n_chunks - 1 - ci) if rev else ci)

    in_specs = [
        pl.BlockSpec((c, kw), lambda b, h, ci: (rowblk(b, h, ci), h)),
        pl.BlockSpec((c, kw), lambda b, h, ci: (rowblk(b, h, ci), k_col0 + h)),
        pl.BlockSpec((c, vw), lambda b, h, ci: (rowblk(b, h, ci), v_col0 + h)),
        pl.BlockSpec((c, rank), lambda b, h, ci: (rowblk(b, h, ci), 0)),
        pl.BlockSpec((rank, kw), lambda b, h, ci: (0, h)),
        pl.BlockSpec((1, kw), lambda b, h, ci: (0, h)),
        pl.BlockSpec((c, c), lambda b, h, ci: (0, 0)),
        pl.BlockSpec((n_lvl, c, c), lambda b, h, ci: (0, 0, 0)),
    ]
    args = [u, u, u, low, gate_b, gate_bias.reshape(1, -1), tri, lmask]
    if rev:
        in_specs += [
            pl.BlockSpec((c, vw), lambda b, h, ci: (rowblk(b, h, ci), h)),
            pl.BlockSpec((c, vw), lambda b, h, ci: (rowblk(b, h, ci), r_col0 + h)),
            pl.BlockSpec((1, dv), lambda b, h, ci: (0, 0)),
        ]
        args += [o_fw, u, head_norm.reshape(1, dv)]
    halo = -(-cfg.gla_sub // SUBLANES) * SUBLANES
    kern = functools.partial(_gla_kernel, rev=rev, chunk=c, sub=cfg.gla_sub, halo=halo,
                             hpb=hpb, dk=dk, dv=dv, n_chunks=n_chunks, n_pad=cfg.n_pad,
                             tau=GLA_GATE_TAU, eps=EPS)
    return pl.pallas_call(
        kern,
        out_shape=jax.ShapeDtypeStruct((m, hh * dv), BF16 if rev else F32),
        grid=(cfg.batch, hh // hpb, n_chunks),
        in_specs=in_specs,
        out_specs=pl.BlockSpec((c, vw), lambda b, h, ci: (rowblk(b, h, ci), h)),
        scratch_shapes=[pltpu.VMEM((hpb, dv, dk), F32),
                        pltpu.VMEM((hpb, c + 2 * halo, dk), F32),
                        pltpu.VMEM((hpb, c + 2 * halo, dk), F32)],
        compiler_params=_params("parallel", "parallel", "arbitrary"),
        name="gla_reverse" if rev else "gla_forward",
    )(*args)


def _rope_tables(cfg):
    hd = cfg.head_dim
    n_freq = hd // 4
    t = jnp.arange(cfg.seq)
    rows = (t // cfg.grid_w).astype(F32)
    cols = (t % cfg.grid_w).astype(F32)
    inv_freq = jnp.power(ROPE_THETA, -jnp.arange(n_freq, dtype=F32) / n_freq)
    ang = jnp.concatenate([jnp.tile(rows[:, None] * inv_freq, (1, 2)),
                           jnp.tile(cols[:, None] * inv_freq, (1, 2))], axis=1)
    ang = jnp.concatenate([jnp.zeros((cfg.n_pad + cfg.n_meta, hd), F32), ang], axis=0)
    sign = jnp.where((jnp.arange(hd) % (hd // 2)) < n_freq, -1.0, 1.0).astype(F32)
    cos_t = jnp.tile(jnp.cos(ang), (cfg.batch, 1))
    sin_t = jnp.tile(jnp.sin(ang) * sign, (cfg.batch, 1))
    return cos_t, sin_t


def _dft_tables(cfg):
    ll, gd, lp, n_pad = cfg.length, cfg.four_dim, cfg.lp, cfg.n_pad
    scale = 1.0 / math.sqrt(ll * gd)
    assert lp % LANES == 0
    n_hi = lp // LANES
    t = jnp.maximum(jnp.arange(lp, dtype=jnp.int32) - n_pad, 0)[None, :]
    k_lo = jnp.arange(LANES, dtype=jnp.int32)[:, None]
    k_hi = jnp.arange(n_hi, dtype=jnp.int32)[:, None] * LANES - n_pad
    ang_lo = (2.0 * math.pi / ll) * ((k_lo * t) % ll).astype(F32)
    ang_hi = (2.0 * math.pi / ll) * ((k_hi * t) % ll).astype(F32)
    c_lo, s_lo = jnp.cos(ang_lo)[None], jnp.sin(ang_lo)[None]
    c_hi, s_hi = (jnp.cos(ang_hi) * scale)[:, None], (jnp.sin(ang_hi) * scale)[:, None]
    row_ok = (jnp.arange(lp) >= n_pad).reshape(n_hi, LANES, 1)
    live = row_ok & (jnp.arange(lp) >= n_pad)[None, None, :]
    wc = jnp.where(live, c_hi * c_lo - s_hi * s_lo, 0.0).astype(BF16).reshape(lp, lp)
    ws = jnp.where(live, s_hi * c_lo + c_hi * s_lo, 0.0).astype(BF16).reshape(lp, lp)
    cc = jnp.arange(gd, dtype=jnp.int32)
    ang_c = (2.0 * math.pi / gd) * ((cc[:, None] * cc[None, :]) % gd).astype(F32)
    cs = jnp.concatenate([jnp.cos(ang_c), jnp.sin(ang_c)], axis=1).astype(BF16)
    return wc, ws, cs


def _forward(cfg, x, meta_tokens, pre_norm, ffn_w_gate, ffn_w_up, ffn_w_down, even_w_in,
             even_q_norm, even_k_norm, even_w_out, odd_w_in, odd_gate_a, odd_gate_b,
             odd_gate_bias, odd_head_norm, odd_w_out):
    bsz, d = cfg.batch, cfg.d_model
    meta = jnp.broadcast_to(meta_tokens.astype(F32)[None], (bsz, cfg.n_meta, d))
    h = jnp.concatenate([jnp.zeros((bsz, cfg.n_pad, d), F32), meta, x.astype(F32)], axis=1)
    h = h.reshape(cfg.rows, d)

    wg, wu, wd = ffn_w_gate.astype(BF16), ffn_w_up.astype(BF16), ffn_w_down.astype(BF16)
    ew_in, ew_out = even_w_in.astype(BF16), even_w_out.astype(BF16)
    ow_in, ow_out = odd_w_in.astype(BF16), odd_w_out.astype(BF16)
    ga = jnp.concatenate([odd_gate_a[:, 0], odd_gate_a[:, 1]], axis=-1).astype(BF16)

    cos_t, sin_t = _rope_tables(cfg)
    wc, ws, cs = _dft_tables(cfg)

    def ffn(h, layer, slot, norm_slot):
        a = _ffn_up(cfg, h, pre_norm[layer, norm_slot], wg, wu, layer, slot)
        return _mm_res(cfg, [a], wd, (layer, slot), h, 0.5)

    for layer in range(cfg.depth):
        i = layer // 2
        h = ffn(h, layer, 0, 0)
        mix_gain = pre_norm[layer, 1]
        if layer % 2 == 0:
            u, qt, vt = _even_inproj(cfg, h, mix_gain, ew_in, i, cos_t, sin_t, even_q_norm[i],
                                     even_k_norm[i])
            attn = _attention(cfg, u, qt, vt)
            fa, fb = _chan_dft(cfg, u, cs)
            four = _seq_dft(cfg, wc, ws, fa, fb)
            h = _mm_res(cfg, [attn, four], ew_out, (i,), h, 1.0)
        else:
            u, low = _odd_inproj(cfg, h, mix_gain, ow_in, i, ga)
            rank = cfg.gla_rank
            o_fw = _gla(cfg, u, low[:, :rank], odd_gate_b[i, 0], odd_gate_bias[i, 0], rev=False)
            og = _gla(cfg, u, low[:, rank:], odd_gate_b[i, 1], odd_gate_bias[i, 1], rev=True,
                      o_fw=o_fw, head_norm=odd_head_norm[i])
            h = _mm_res(cfg, [og], ow_out, (i,), h, 1.0)
        h = ffn(h, layer, 1, 2)
    return h.reshape(bsz, cfg.lp, d)[:, cfg.n_pad + cfg.n_meta:]


def kernel(x, meta_tokens, pre_norm, ffn_w_gate, ffn_w_up, ffn_w_down, even_w_in, even_q_norm,
           even_k_norm, even_w_out, odd_w_in, odd_gate_a, odd_gate_b, odd_gate_bias,
           odd_head_norm, odd_w_out):
    bsz, seq, d = x.shape
    cfg = Cfg(batch=bsz, seq=seq, d_model=d, d_ff=ffn_w_gate.shape[-1], depth=pre_norm.shape[0],
              n_meta=N_META, grid_w=GRID_W, attn_heads=ATTN_HEADS, attn_kv_heads=ATTN_KV_HEADS,
              head_dim=HEAD_DIM, q_block=Q_BLOCK, four_groups=FOURIER_GROUPS,
              four_dim=FOURIER_GROUP_DIM, gla_heads=GLA_HEADS, gla_rank=odd_gate_a.shape[-1],
              tm_res=1280, tm=640, tn=512, attn_tk=1024, attn_unroll=7, dft_tm=1664, dft_tn=1024, dft_tk=640,
              gla_chunk=128, gla_sub=4, gla_heads_per_step=8)
    return _forward(cfg, x, meta_tokens, pre_norm, ffn_w_gate, ffn_w_up, ffn_w_down, even_w_in,
                    even_q_norm, even_k_norm, even_w_out, odd_w_in, odd_gate_a, odd_gate_b,
                    odd_gate_bias, odd_head_norm, odd_w_out)
```

```python
import functools
import math
from typing import NamedTuple

import jax
import jax.numpy as jnp
from jax import lax
from jax.experimental import pallas as pl
from jax.experimental.pallas import tpu as pltpu

F32 = jnp.float32
BF16 = jnp.bfloat16

N_META = 16
GRID_W = 64
EPS = 1e-6
ATTN_HEADS = 16
ATTN_KV_HEADS = 4
HEAD_DIM = 128
Q_BLOCK = 128
ROPE_THETA = 10000.0
FOURIER_GROUPS = 16
FOURIER_GROUP_DIM = 128
GLA_HEADS = 8
GLA_GATE_TAU = 16.0

V7X_VMEM_BYTES = 64 * 1024 * 1024
VMEM_LIMIT_BYTES = V7X_VMEM_BYTES - 8 * 1024 * 1024
LANES = 128
SUBLANES = 8
NEG_BIG = -1e30


class Cfg(NamedTuple):
    batch: int
    seq: int
    d_model: int
    d_ff: int
    depth: int
    n_meta: int
    grid_w: int
    attn_heads: int
    attn_kv_heads: int
    head_dim: int
    q_block: int
    four_groups: int
    four_dim: int
    gla_heads: int
    gla_rank: int
    tm_res: int
    tm: int
    tn: int
    attn_tk: int
    attn_unroll: int
    dft_tm: int
    dft_tn: int
    dft_tk: int
    gla_chunk: int
    gla_sub: int
    gla_heads_per_step: int

    @property
    def length(self):
        return self.n_meta + self.seq

    @property
    def n_pad(self):
        return (-self.length) % self.q_block

    @property
    def lp(self):
        return self.length + self.n_pad

    @property
    def rows(self):
        return self.batch * self.lp

    @property
    def q_w(self):
        return self.attn_heads * self.head_dim

    @property
    def kv_w(self):
        return self.attn_kv_heads * self.head_dim

    @property
    def four_w(self):
        return self.four_groups * self.four_dim

    @property
    def gla_dk(self):
        return self.d_model // (2 * self.gla_heads)

    @property
    def gla_dv(self):
        return self.d_model // self.gla_heads


def _params(*sem):
    return pltpu.CompilerParams(dimension_semantics=sem, vmem_limit_bytes=VMEM_LIMIT_BYTES)


def _dot(a, b):
    return jnp.dot(a, b, preferred_element_type=F32)


def _dot_nt(a, b):
    return lax.dot_general(a, b, (((1,), (1,)), ((), ())), preferred_element_type=F32)


def _dot_tn(a, b):
    return lax.dot_general(a, b, (((0,), (0,)), ((), ())), preferred_element_type=F32)


NORM_ROWS = 16
NORM_UNROLL = 8


def _normalize_rows(h_ref, g_ref, xn_ref, eps):
    @pl.when(pl.program_id(1) == 0)
    def _():
        gain = g_ref[...]

        def body(r, carry):
            rows = pl.ds(pl.multiple_of(r * NORM_ROWS, NORM_ROWS), NORM_ROWS)
            x = h_ref[rows, :]
            ms = jnp.mean(x * x, axis=-1, keepdims=True)
            xn_ref[rows, :] = (x * lax.rsqrt(ms + eps) * gain).astype(xn_ref.dtype)
            return carry

        lax.fori_loop(0, h_ref.shape[0] // NORM_ROWS, body, 0, unroll=NORM_UNROLL)


def _norm_specs(tm, d):
    return [pl.BlockSpec((tm, d), lambda i, j: (i, 0)), pl.BlockSpec((1, d), lambda i, j: (0, 0))]


def _ffn_up_kernel(h_ref, g_ref, wg_ref, wu_ref, o_ref, xn_ref, *, eps):
    _normalize_rows(h_ref, g_ref, xn_ref, eps)
    x = xn_ref[...]
    g = _dot(x, wg_ref[...])
    u = _dot(x, wu_ref[...])
    o_ref[...] = (g * jax.nn.sigmoid(g) * u).astype(o_ref.dtype)


def _ffn_up(cfg, h, gain, wg, wu, layer, slot):
    m, d = h.shape
    f = wg.shape[-1]
    tm, tn = cfg.tm, cfg.tn
    assert tm % NORM_ROWS == 0
    wspec = pl.BlockSpec((None, None, d, tn), lambda i, j: (layer, slot, 0, j))
    return pl.pallas_call(
        functools.partial(_ffn_up_kernel, eps=EPS),
        out_shape=jax.ShapeDtypeStruct((m, f), BF16),
        grid=(m // tm, f // tn),
        in_specs=_norm_specs(tm, d) + [wspec, wspec],
        out_specs=pl.BlockSpec((tm, tn), lambda i, j: (i, j)),
        scratch_shapes=[pltpu.VMEM((tm, d), BF16)],
        compiler_params=_params("parallel", "arbitrary"),
        name="ffn_up",
    )(h, gain.reshape(1, d), wg, wu)


def _mm_res_kernel(*refs, n_lhs, scale):
    xs, ws = refs[:n_lhs], refs[n_lhs:2 * n_lhs]
    h_ref, o_ref = refs[2 * n_lhs], refs[2 * n_lhs + 1]
    acc = _dot(xs[0][...], ws[0][...])
    for x_ref, w_ref in zip(xs[1:], ws[1:]):
        acc = acc + _dot(x_ref[...], w_ref[...])
    o_ref[...] = h_ref[...] + scale * acc


def _mm_res(cfg, xs, w, w_lead, h, scale):
    m, d = h.shape
    kx = xs[0].shape[1]
    n_lhs = len(xs)
    tm, tn = cfg.tm_res, cfg.tn
    n_lead = len(w_lead)
    w_specs = [
        pl.BlockSpec((None,) * n_lead + (kx, tn), lambda i, j, p=p: tuple(w_lead) + (p, j))
        for p in range(n_lhs)
    ]
    return pl.pallas_call(
        functools.partial(_mm_res_kernel, n_lhs=n_lhs, scale=scale),
        out_shape=jax.ShapeDtypeStruct((m, d), F32),
        grid=(m // tm, d // tn),
        in_specs=[pl.BlockSpec((tm, kx), lambda i, j: (i, 0))] * n_lhs + w_specs
        + [pl.BlockSpec((tm, tn), lambda i, j: (i, j))],
        out_specs=pl.BlockSpec((tm, tn), lambda i, j: (i, j)),
        input_output_aliases={2 * n_lhs: 0},
        compiler_params=_params("parallel", "arbitrary"),
        name="mm_residual",
    )(*xs, *([w] * n_lhs), h)


def _even_inproj_kernel(h_ref, g_ref, w_ref, cos_ref, sin_ref, qg_ref, kg_ref, o_ref, qt_ref,
                        vt_ref, xn_ref, *, n_q_tiles, n_k_tiles, hd, eps, q_scale):
    _normalize_rows(h_ref, g_ref, xn_ref, eps)
    j = pl.program_id(1)
    acc = _dot(xn_ref[...], w_ref[...])
    heads_per_tile = acc.shape[1] // hd

    def normed_rotated(gain):
        c = cos_ref[...]
        s = sin_ref[...]
        lane = lax.broadcasted_iota(jnp.int32, c.shape, 1)
        first_half = (lane % (hd // 2)) < (hd // 4)
        for hh in range(heads_per_tile):
            xh = acc[:, hh * hd:(hh + 1) * hd]
            ms = jnp.mean(xh * xh, axis=-1, keepdims=True)
            y = xh * lax.rsqrt(ms + eps) * gain
            partner = jnp.where(first_half,
                                pltpu.roll(y, hd - hd // 4, axis=1),
                                pltpu.roll(y, hd // 4, axis=1))
            yield hh, y * c + partner * s

    @pl.when(j < n_q_tiles)
    def _():
        for hh, y in normed_rotated(qg_ref[...] * q_scale):
            o_ref[:, hh * hd:(hh + 1) * hd] = y.astype(o_ref.dtype)
            qt_ref[hh * hd:(hh + 1) * hd, :] = y.T.astype(qt_ref.dtype)

    @pl.when((j >= n_q_tiles) & (j < n_q_tiles + n_k_tiles))
    def _():
        for hh, y in normed_rotated(kg_ref[...]):
            o_ref[:, hh * hd:(hh + 1) * hd] = y.astype(o_ref.dtype)

    @pl.when(j >= n_q_tiles + n_k_tiles)
    def _():
        o_ref[...] = acc.astype(o_ref.dtype)

    @pl.when((j >= n_q_tiles + n_k_tiles) & (j < n_q_tiles + 2 * n_k_tiles))
    def _():
        vt_ref[...] = acc.T.astype(vt_ref.dtype)


def _even_inproj(cfg, h, gain, w, layer_i, cos_t, sin_t, q_gain, k_gain):
    m, d = h.shape
    n = w.shape[-1]
    tm, tn, hd = cfg.tm, cfg.tn, cfg.head_dim
    assert cfg.q_w % tn == 0 and cfg.kv_w % tn == 0 and tn % hd == 0 and tm % NORM_ROWS == 0
    n_q, n_k = cfg.q_w // tn, cfg.kv_w // tn
    kern = functools.partial(
        _even_inproj_kernel, n_q_tiles=n_q, n_k_tiles=n_k, hd=hd,
        eps=EPS, q_scale=hd ** -0.5 * math.log2(math.e))
    return pl.pallas_call(
        kern,
        out_shape=(jax.ShapeDtypeStruct((m, n), BF16),
                   jax.ShapeDtypeStruct((cfg.q_w, m), BF16),
                   jax.ShapeDtypeStruct((cfg.kv_w, m), BF16)),
        grid=(m // tm, n // tn),
        in_specs=_norm_specs(tm, d) + [
            pl.BlockSpec((None, d, tn), lambda i, j: (layer_i, 0, j)),
            pl.BlockSpec((tm, hd), lambda i, j: (i, 0)),
            pl.BlockSpec((tm, hd), lambda i, j: (i, 0)),
            pl.BlockSpec((1, hd), lambda i, j: (0, 0)),
            pl.BlockSpec((1, hd), lambda i, j: (0, 0))],
        out_specs=(
            pl.BlockSpec((tm, tn), lambda i, j: (i, j)),
            pl.BlockSpec((tn, tm), lambda i, j: (jnp.minimum(j, n_q - 1), i)),
            pl.BlockSpec((tn, tm), lambda i, j: (jnp.clip(j - n_q - n_k, 0, n_k - 1), i))),
        scratch_shapes=[pltpu.VMEM((tm, d), BF16)],
        compiler_params=_params("parallel", "arbitrary"),
        name="even_inproj",
    )(h, gain.reshape(1, d), w, cos_t, sin_t, q_gain.reshape(1, hd), k_gain.reshape(1, hd))


def _attn_kernel(qt_ref, k_ref, vt_ref, o_ref, acc_ref, *, groups, hpc, hd, head, sub, n_sub,
                 n_pad, unroll):
    tq = qt_ref.shape[1]
    n_chain = groups // hpc
    cw = hpc * tq
    q_t = [jnp.concatenate([qt_ref[(c * hpc + g) * hd:(c * hpc + g + 1) * hd, :]
                            for g in range(hpc)], axis=1) for c in range(n_chain)]

    def scores(start, size):
        k_blk = k_ref[pl.ds(start, size), :]
        return tuple(_dot(k_blk, q_t[c]) for c in range(n_chain))

    def consume(start, size, st, ml, masked):
        v_blk = vt_ref[:, pl.ds(start, size)]
        out = []
        for c in range(n_chain):
            m_prev, l_prev = ml[c]
            s = st[c]
            if masked:
                row = lax.broadcasted_iota(jnp.int32, s.shape, 0)
                s = jnp.where(row < n_pad, NEG_BIG, s)
            m_new = jnp.maximum(m_prev, jnp.max(s, axis=0, keepdims=True))
            alpha = jnp.exp2(m_prev - m_new)
            p = jnp.exp2(s - m_new)
            l_new = alpha * l_prev + jnp.sum(p, axis=0, keepdims=True)
            acc_ref[c] = alpha * acc_ref[c] + _dot(v_blk, p.astype(v_blk.dtype))
            out.append((m_new, l_new))
        return tuple(out)

    acc_ref[...] = jnp.zeros_like(acc_ref)
    ml = tuple((jnp.full((1, cw), NEG_BIG, F32), jnp.zeros((1, cw), F32))
               for _ in range(n_chain))
    def sub_start(j):
        return pl.multiple_of(head + j * sub, LANES)

    st = scores(0, head)
    st_next = scores(sub_start(0), sub)
    ml = consume(0, head, st, ml, True)

    def body(it, carry):
        st, ml = carry
        for u in range(unroll):
            j = it * unroll + u
            st_next = scores(sub_start(jnp.minimum(j + 1, n_sub - 1)), sub)
            ml = consume(sub_start(j), sub, st, ml, False)
            st = st_next
        return st, ml

    _, ml = lax.fori_loop(0, n_sub // unroll, body, (st_next, ml))
    for c in range(n_chain):
        o_t = acc_ref[c] / ml[c][1]
        for g in range(hpc):
            h0 = (c * hpc + g) * hd
            o_ref[:, h0:h0 + hd] = o_t[:, g * tq:(g + 1) * tq].T.astype(o_ref.dtype)


def _attention(cfg, u, qt, vt):
    m = u.shape[0]
    hd, kv, tq, lp = cfg.head_dim, cfg.attn_kv_heads, cfg.q_block, cfg.lp
    groups = cfg.attn_heads // kv
    hpc = min(groups, 2)
    n_q = lp // tq
    sub = cfg.attn_tk
    head = max(-(-cfg.n_pad // LANES), 1) * LANES
    n_sub = (lp - head) // sub
    assert (lp - head) % sub == 0 and n_sub % cfg.attn_unroll == 0 and sub % LANES == 0
    kern = functools.partial(_attn_kernel, groups=groups, hpc=hpc, hd=hd, head=head, sub=sub,
                             n_sub=n_sub, n_pad=cfg.n_pad, unroll=cfg.attn_unroll)
    k_col0 = cfg.q_w // hd
    return pl.pallas_call(
        kern,
        out_shape=jax.ShapeDtypeStruct((m, cfg.q_w), BF16),
        grid=(cfg.batch, kv, n_q),
        in_specs=[pl.BlockSpec((groups * hd, tq), lambda b, h, i: (h, b * n_q + i)),
                  pl.BlockSpec((lp, hd), lambda b, h, i: (b, k_col0 + h)),
                  pl.BlockSpec((hd, lp), lambda b, h, i: (h, b))],
        out_specs=pl.BlockSpec((tq, groups * hd), lambda b, h, i: (b * n_q + i, h)),
        scratch_shapes=[pltpu.VMEM((groups // hpc, hd, hpc * tq), F32)],
        compiler_params=_params("parallel", "parallel", "arbitrary"),
        name="gqa_attention",
    )(qt, u, vt)


def _chan_dft_kernel(f_ref, cs_ref, a_ref, b_ref, *, gd):
    cs = cs_ref[...]
    for g in range(f_ref.shape[1] // gd):
        ab = _dot(f_ref[:, g * gd:(g + 1) * gd], cs)
        a_ref[:, g * gd:(g + 1) * gd] = ab[:, :gd].astype(a_ref.dtype)
        b_ref[:, g * gd:(g + 1) * gd] = ab[:, gd:].astype(b_ref.dtype)


def _chan_dft(cfg, u, cs):
    m = u.shape[0]
    gd, fw = cfg.four_dim, cfg.four_w
    f0 = cfg.q_w + 2 * cfg.kv_w
    fb = math.gcd(math.gcd(f0, fw), 1024)
    tm = cfg.tm
    out = jax.ShapeDtypeStruct((m, fw), BF16)
    return pl.pallas_call(
        functools.partial(_chan_dft_kernel, gd=gd),
        out_shape=(out, out),
        grid=(m // tm, fw // fb),
        in_specs=[pl.BlockSpec((tm, fb), lambda i, j: (i, f0 // fb + j)),
                  pl.BlockSpec((gd, 2 * gd), lambda i, j: (0, 0))],
        out_specs=(pl.BlockSpec((tm, fb), lambda i, j: (i, j)),
                   pl.BlockSpec((tm, fb), lambda i, j: (i, j))),
        compiler_params=_params("parallel", "parallel"),
        name="fourier_channel_dft",
    )(u, cs)


def _seq_dft_kernel(wc_ref, ws_ref, a_ref, b_ref, d_ref, s_ref, accp_ref, accq_ref):
    k = pl.program_id(3)

    @pl.when(k == 0)
    def _():
        accp_ref[...] = jnp.zeros_like(accp_ref)
        accq_ref[...] = jnp.zeros_like(accq_ref)

    accp_ref[...] += _dot(wc_ref[...], a_ref[...])
    accq_ref[...] += _dot(ws_ref[...], b_ref[...])

    @pl.when(k == pl.num_programs(3) - 1)
    def _():
        p, q = accp_ref[...], accq_ref[...]
        d_ref[...] = (p - q).astype(d_ref.dtype)
        s_ref[...] = (p + q).astype(s_ref.dtype)


def _seq_dft_half(cfg, wc, ws, a, b):
    fw = a.shape[1]
    hh = wc.shape[0]
    lp = cfg.lp
    tm, tn, tk = cfg.dft_tm, cfg.dft_tn, cfg.dft_tk
    assert hh % tm == 0 and lp % tk == 0 and fw % tn == 0
    n_m, n_k = hh // tm, lp // tk
    out = jax.ShapeDtypeStruct((cfg.batch * hh, fw), BF16)
    return pl.pallas_call(
        _seq_dft_kernel,
        out_shape=(out, out),
        grid=(cfg.batch, fw // tn, n_m, n_k),
        in_specs=[pl.BlockSpec((tm, tk), lambda bb, n, i, k: (i, k)),
                  pl.BlockSpec((tm, tk), lambda bb, n, i, k: (i, k)),
                  pl.BlockSpec((tk, tn), lambda bb, n, i, k: (bb * n_k + k, n)),
                  pl.BlockSpec((tk, tn), lambda bb, n, i, k: (bb * n_k + k, n))],
        out_specs=(pl.BlockSpec((tm, tn), lambda bb, n, i, k: (bb * n_m + i, n)),
                   pl.BlockSpec((tm, tn), lambda bb, n, i, k: (bb * n_m + i, n))),
        scratch_shapes=[pltpu.VMEM((tm, tn), F32), pltpu.VMEM((tm, tn), F32)],
        compiler_params=_params("parallel", "parallel", "parallel", "arbitrary"),
        name="fourier_sequence_dft",
    )(wc, ws, a, b)


def _four_assemble_kernel(d_ref, s1_ref, s2_ref, j1_ref, j2_ref, o_ref, *, r_split):
    blk = pl.program_id(1)
    row = blk * LANES + lax.broadcasted_iota(jnp.int32, (LANES, 1), 0)
    rev = _dot(j1_ref[...], s1_ref[...]) + _dot(j2_ref[...], s2_ref[...])
    o_ref[...] = jnp.where(row < r_split, d_ref[...].astype(F32), rev).astype(o_ref.dtype)


def _four_assemble(cfg, dd, ss):
    fw = dd.shape[1]
    lp, n_pad = cfg.lp, cfg.n_pad
    n_blk = lp // LANES
    n_h = dd.shape[0] // cfg.batch // LANES
    r_split = n_pad + cfg.length // 2 + 1
    tn = cfg.dft_tn
    s_idx = jnp.arange(LANES)[:, None]
    o_idx = jnp.arange(LANES)[None, :]
    j1 = ((s_idx <= n_pad) & (o_idx == n_pad - s_idx)).astype(BF16)
    j2 = ((s_idx > n_pad) & (o_idx == LANES + n_pad - s_idx)).astype(BF16)

    def half_blk(bb, idx):
        return bb * n_h + jnp.clip(idx, 0, n_h - 1)

    return pl.pallas_call(
        functools.partial(_four_assemble_kernel, r_split=r_split),
        out_shape=jax.ShapeDtypeStruct((cfg.batch * lp, fw), BF16),
        grid=(cfg.batch, n_blk, fw // tn),
        in_specs=[pl.BlockSpec((LANES, tn), lambda bb, i, n: (half_blk(bb, i), n)),
                  pl.BlockSpec((LANES, tn), lambda bb, i, n: (half_blk(bb, n_blk - i), n)),
                  pl.BlockSpec((LANES, tn), lambda bb, i, n: (half_blk(bb, n_blk - i - 1), n)),
                  pl.BlockSpec((LANES, LANES), lambda bb, i, n: (0, 0)),
                  pl.BlockSpec((LANES, LANES), lambda bb, i, n: (0, 0))],
        out_specs=pl.BlockSpec((LANES, tn), lambda bb, i, n: (bb * n_blk + i, n)),
        compiler_params=_params("parallel", "parallel", "parallel"),
        name="fourier_assemble",
    )(dd, ss, ss, j1, j2)


def _odd_inproj_kernel(h_ref, g_ref, w_ref, ga_ref, o_ref, low_ref, xn_ref, *, n_q_tiles,
                       q_scale, eps):
    _normalize_rows(h_ref, g_ref, xn_ref, eps)

    @pl.when(pl.program_id(1) == 0)
    def _():
        low_ref[...] = _dot(xn_ref[...], ga_ref[...])

    acc = _dot(xn_ref[...], w_ref[...])
    sc = jnp.where(pl.program_id(1) < n_q_tiles, q_scale, 1.0).astype(F32)
    o_ref[...] = (acc * sc).astype(o_ref.dtype)


def _odd_inproj(cfg, h, gain, w, layer_i, ga):
    m, d = h.shape
    n = w.shape[-1]
    r2 = ga.shape[-1]
    tm, tn = cfg.tm, cfg.tn
    assert tm % NORM_ROWS == 0
    kern = functools.partial(_odd_inproj_kernel, n_q_tiles=cfg.gla_heads * cfg.gla_dk // tn,
                             q_scale=cfg.gla_dk ** -0.5, eps=EPS)
    return pl.pallas_call(
        kern,
        out_shape=(jax.ShapeDtypeStruct((m, n), BF16), jax.ShapeDtypeStruct((m, r2), F32)),
        grid=(m // tm, n // tn),
        in_specs=_norm_specs(tm, d) + [
            pl.BlockSpec((None, d, tn), lambda i, j: (layer_i, 0, j)),
            pl.BlockSpec((None, d, r2), lambda i, j: (layer_i, 0, 0))],
        out_specs=(pl.BlockSpec((tm, tn), lambda i, j: (i, j)),
                   pl.BlockSpec((tm, r2), lambda i, j: (i, 0))),
        scratch_shapes=[pltpu.VMEM((tm, d), BF16)],
        compiler_params=_params("parallel", "arbitrary"),
        name="odd_inproj",
    )(h, gain.reshape(1, d), w, ga)


def _gla_kernel(*refs, rev, chunk, sub, halo, hpb, dk, dv, n_chunks, n_pad, tau, eps):
    if rev:
        (q_ref, k_ref, v_ref, low_ref, gb_ref, bias_ref, tri_ref, lmask_ref,
         ofw_ref, r_ref, hn_ref, o_ref, s_ref, cump_ref, kfp_ref) = refs
    else:
        (q_ref, k_ref, v_ref, low_ref, gb_ref, bias_ref, tri_ref, lmask_ref,
         o_ref, s_ref, cump_ref, kfp_ref) = refs
    c = pl.program_id(2)
    pos_chunk = (n_chunks - 1 - c) if rev else c

    @pl.when(c == 0)
    def _():
        s_ref[...] = jnp.zeros_like(s_ref)
        cump_ref[...] = jnp.zeros_like(cump_ref)
        kfp_ref[...] = jnp.zeros_like(kfp_ref)

    row = pos_chunk * chunk + lax.broadcasted_iota(jnp.int32, (chunk, 1), 0)
    valid = row >= n_pad
    col_minus_row = (lax.broadcasted_iota(jnp.int32, (chunk, chunk), 1)
                     - lax.broadcasted_iota(jnp.int32, (chunk, chunk), 0))
    row_in_sub = lax.broadcasted_iota(jnp.int32, (chunk, 1), 0) % sub
    low = low_ref[...]
    tri = tri_ref[...]

    def head(hx):
        ks = slice(hx * dk, (hx + 1) * dk)
        vs = slice(hx * dv, (hx + 1) * dv)
        z = jnp.dot(low, gb_ref[:, ks], preferred_element_type=F32,
                    precision=lax.Precision.HIGHEST) + bias_ref[:, ks]
        log_a = (jnp.minimum(z, 0.0) - jnp.log(1.0 + jnp.exp(-jnp.abs(z)))) * (1.0 / tau)
        log_a = jnp.where(valid, log_a, 0.0)
        kf = jnp.where(valid, k_ref[:, ks].astype(F32), 0.0)
        qf = q_ref[:, ks].astype(F32)
        v = v_ref[:, vs]

        g_hi = log_a.astype(BF16)
        g_lo = (log_a - g_hi.astype(F32)).astype(BF16)
        cum = _dot(tri, g_hi) + _dot(tri, g_lo)
        cum_end = cum[0:1] if rev else cum[chunk - 1:chunk]
        yield

        s_old = s_ref[hx]
        inter = _dot_nt((qf * jnp.exp(cum)).astype(BF16), s_old.astype(BF16))
        yield

        a_acc = jnp.zeros((chunk, chunk), F32)
        s_blk, level = chunk // 2, 0
        while s_blk >= sub:
            grp = 2 * s_blk
            ref_idx = s_blk if rev else s_blk - 1
            ref = jnp.broadcast_to(
                cum.reshape(chunk // grp, grp, dk)[:, ref_idx:ref_idx + 1, :],
                (chunk // grp, grp, dk)).reshape(chunk, dk)
            e = jnp.exp(-jnp.abs(cum - ref))
            a_lvl = _dot_nt((qf * e).astype(BF16), (kf * e).astype(BF16))
            a_acc = a_acc + a_lvl * lmask_ref[level]
            s_blk //= 2
            level += 1
            yield

        cump_ref[hx, halo:halo + chunk, :] = cum
        kfp_ref[hx, halo:halo + chunk, :] = kf
        a_acc = a_acc + jnp.where(col_minus_row == 0,
                                  jnp.sum(qf * kf, axis=-1, keepdims=True), 0.0)
        for delta in range(1, sub):
            off = halo + delta if rev else halo - delta
            k_sh = kfp_ref[hx, off:off + chunk, :]
            cum_sh = cump_ref[hx, off:off + chunk, :]
            col = jnp.sum(qf * k_sh * jnp.exp(cum - cum_sh), axis=-1, keepdims=True)
            same_sub = (row_in_sub + delta < sub) if rev else (row_in_sub >= delta)
            col = jnp.where(same_sub, col, 0.0)
            a_acc = a_acc + jnp.where(col_minus_row == (delta if rev else -delta), col, 0.0)
            yield

        o = inter + _dot(a_acc.astype(BF16), v)

        k_out = (kf * jnp.exp(cum_end - cum)).astype(BF16)
        s_ref[hx] = s_old * jnp.exp(cum_end) + _dot_tn(v, k_out)
        yield

        if rev:
            tot = ofw_ref[:, vs] + o
            ms = jnp.mean(tot * tot, axis=-1, keepdims=True)
            y = tot * lax.rsqrt(ms + eps) * hn_ref[...]
            r = r_ref[:, vs].astype(F32)
            o_ref[:, vs] = (y * r * jax.nn.sigmoid(r)).astype(o_ref.dtype)
        else:
            o_ref[:, vs] = o

    chains = [head(hx) for hx in range(hpb)]
    while chains:
        for g in list(chains):
            if next(g, "done") == "done":
                chains.remove(g)


def _gla_constants(cfg, rev):
    c, sub = cfg.gla_chunk, cfg.gla_sub
    i = jnp.arange(c)[:, None]
    j = jnp.arange(c)[None, :]
    tri = ((j >= i) if rev else (j <= i)).astype(BF16)
    masks = []
    s = c // 2
    while s >= sub:
        i_hi, j_hi = (i // s) % 2 == 1, (j // s) % 2 == 1
        roles = (~i_hi & j_hi) if rev else (i_hi & ~j_hi)
        masks.append(((i // (2 * s) == j // (2 * s)) & roles).astype(F32))
        s //= 2
    return tri, jnp.stack(masks)


def _gla(cfg, u, low, gate_b, gate_bias, rev, o_fw=None, head_norm=None):
    m = u.shape[0]
    hh, dk, dv, c = cfg.gla_heads, cfg.gla_dk, cfg.gla_dv, cfg.gla_chunk
    rank = cfg.gla_rank
    n_chunks = cfg.lp // c
    tri, lmask = _gla_constants(cfg, rev)
    n_lvl = lmask.shape[0]
    hpb = cfg.gla_heads_per_step
    kw, vw = hpb * dk, hpb * dv
    assert hh % hpb == 0 and (2 * hh * dk) % vw == 0
    k_col0 = hh // hpb
    v_col0 = 2 * hh * dk // vw
    r_col0 = v_col0 + hh // hpb

    def rowblk(b, h, ci):
        return b * n_chunks + ((n_chunks - 1 - ci) if rev else ci)

    in_specs = [
        pl.BlockSpec((c, kw), lambda b, h, ci: (rowblk(b, h, ci), h)),
        pl.BlockSpec((c, kw), lambda b, h, ci: (rowblk(b, h, ci), k_col0 + h)),
        pl.BlockSpec((c, vw), lambda b, h, ci: (rowblk(b, h, ci), v_col0 + h)),
        pl.BlockSpec((c, rank), lambda b, h, ci: (rowblk(b, h, ci), 0)),
        pl.BlockSpec((rank, kw), lambda b, h, ci: (0, h)),
        pl.BlockSpec((1, kw), lambda b, h, ci: (0, h)),
        pl.BlockSpec((c, c), lambda b, h, ci: (0, 0)),
        pl.BlockSpec((n_lvl, c, c), lambda b, h, ci: (0, 0, 0)),
    ]
    args = [u, u, u, low, gate_b, gate_bias.reshape(1, -1), tri, lmask]
    if rev:
        in_specs += [
            pl.BlockSpec((c, vw), lambda b, h, ci: (rowblk(b, h, ci), h)),
            pl.BlockSpec((c, vw), lambda b, h, ci: (rowblk(b, h, ci), r_col0 + h)),
            pl.BlockSpec((1, dv), lambda b, h, ci: (0, 0)),
        ]
        args += [o_fw, u, head_norm.reshape(1, dv)]
    halo = -(-cfg.gla_sub // SUBLANES) * SUBLANES
    kern = functools.partial(_gla_kernel, rev=rev, chunk=c, sub=cfg.gla_sub, halo=halo,
                             hpb=hpb, dk=dk, dv=dv, n_chunks=n_chunks, n_pad=cfg.n_pad,
                             tau=GLA_GATE_TAU, eps=EPS)
    return pl.pallas_call(
        kern,
        out_shape=jax.ShapeDtypeStruct((m, hh * dv), BF16 if rev else F32),
        grid=(cfg.batch, hh // hpb, n_chunks),
        in_specs=in_specs,
        out_specs=pl.BlockSpec((c, vw), lambda b, h, ci: (rowblk(b, h, ci), h)),
        scratch_shapes=[pltpu.VMEM((hpb, dv, dk), F32),
                        pltpu.VMEM((hpb, c + 2 * halo, dk), F32),
                        pltpu.VMEM((hpb, c + 2 * halo, dk), F32)],
        compiler_params=_params("parallel", "parallel", "arbitrary"),
        name="gla_reverse" if rev else "gla_forward",
    )(*args)


def _rope_tables(cfg):
    hd = cfg.head_dim
    n_freq = hd // 4
    t = jnp.arange(cfg.seq)
    rows = (t // cfg.grid_w).astype(F32)
    cols = (t % cfg.grid_w).astype(F32)
    inv_freq = jnp.power(ROPE_THETA, -jnp.arange(n_freq, dtype=F32) / n_freq)
    ang = jnp.concatenate([jnp.tile(rows[:, None] * inv_freq, (1, 2)),
                           jnp.tile(cols[:, None] * inv_freq, (1, 2))], axis=1)
    ang = jnp.concatenate([jnp.zeros((cfg.n_pad + cfg.n_meta, hd), F32), ang], axis=0)
    sign = jnp.where((jnp.arange(hd) % (hd // 2)) < n_freq, -1.0, 1.0).astype(F32)
    cos_t = jnp.tile(jnp.cos(ang), (cfg.batch, 1))
    sin_t = jnp.tile(jnp.sin(ang) * sign, (cfg.batch, 1))
    return cos_t, sin_t


def _dft_tables(cfg):
    ll, gd, lp, n_pad = cfg.length, cfg.four_dim, cfg.lp, cfg.n_pad
    scale = 1.0 / math.sqrt(ll * gd)
    assert lp % LANES == 0 and ll % 2 == 0 and cfg.dft_tm % LANES == 0
    r_split = n_pad + ll // 2 + 1
    hh = -(-r_split // cfg.dft_tm) * cfg.dft_tm
    n_hi = hh // LANES
    t = jnp.maximum(jnp.arange(lp, dtype=jnp.int32) - n_pad, 0)[None, :]
    k_lo = jnp.arange(LANES, dtype=jnp.int32)[:, None]
    k_hi = jnp.arange(n_hi, dtype=jnp.int32)[:, None] * LANES - n_pad
    ang_lo = (2.0 * math.pi / ll) * ((k_lo * t) % ll).astype(F32)
    ang_hi = (2.0 * math.pi / ll) * ((k_hi * t) % ll).astype(F32)
    c_lo, s_lo = jnp.cos(ang_lo)[None], jnp.sin(ang_lo)[None]
    c_hi, s_hi = (jnp.cos(ang_hi) * scale)[:, None], (jnp.sin(ang_hi) * scale)[:, None]
    rows = jnp.arange(hh)
    row_ok = ((rows >= n_pad) & (rows < r_split)).reshape(n_hi, LANES, 1)
    live = row_ok & (jnp.arange(lp) >= n_pad)[None, None, :]
    wc = jnp.where(live, c_hi * c_lo - s_hi * s_lo, 0.0).astype(BF16).reshape(hh, lp)
    ws = jnp.where(live, s_hi * c_lo + c_hi * s_lo, 0.0).astype(BF16).reshape(hh, lp)
    cc = jnp.arange(gd, dtype=jnp.int32)
    ang_c = (2.0 * math.pi / gd) * ((cc[:, None] * cc[None, :]) % gd).astype(F32)
    cs = jnp.concatenate([jnp.cos(ang_c), jnp.sin(ang_c)], axis=1).astype(BF16)
    return wc, ws, cs


def _forward(cfg, x, meta_tokens, pre_norm, ffn_w_gate, ffn_w_up, ffn_w_down, even_w_in,
             even_q_norm, even_k_norm, even_w_out, odd_w_in, odd_gate_a, odd_gate_b,
             odd_gate_bias, odd_head_norm, odd_w_out):
    bsz, d = cfg.batch, cfg.d_model
    meta = jnp.broadcast_to(meta_tokens.astype(F32)[None], (bsz, cfg.n_meta, d))
    h = jnp.concatenate([jnp.zeros((bsz, cfg.n_pad, d), F32), meta, x.astype(F32)], axis=1)
    h = h.reshape(cfg.rows, d)

    wg, wu, wd = ffn_w_gate.astype(BF16), ffn_w_up.astype(BF16), ffn_w_down.astype(BF16)
    ew_in, ew_out = even_w_in.astype(BF16), even_w_out.astype(BF16)
    ow_in, ow_out = odd_w_in.astype(BF16), odd_w_out.astype(BF16)
    ga = jnp.concatenate([odd_gate_a[:, 0], odd_gate_a[:, 1]], axis=-1).astype(BF16)

    cos_t, sin_t = _rope_tables(cfg)
    wc, ws, cs = _dft_tables(cfg)

    def ffn(h, layer, slot, norm_slot):
        a = _ffn_up(cfg, h, pre_norm[layer, norm_slot], wg, wu, layer, slot)
        return _mm_res(cfg, [a], wd, (layer, slot), h, 0.5)

    for layer in range(cfg.depth):
        i = layer // 2
        h = ffn(h, layer, 0, 0)
        mix_gain = pre_norm[layer, 1]
        if layer % 2 == 0:
            u, qt, vt = _even_inproj(cfg, h, mix_gain, ew_in, i, cos_t, sin_t, even_q_norm[i],
                                     even_k_norm[i])
            attn = _attention(cfg, u, qt, vt)
            fa, fb = _chan_dft(cfg, u, cs)
            four = _four_assemble(cfg, *_seq_dft_half(cfg, wc, ws, fa, fb))
            h = _mm_res(cfg, [attn, four], ew_out, (i,), h, 1.0)
        else:
            u, low = _odd_inproj(cfg, h, mix_gain, ow_in, i, ga)
            rank = cfg.gla_rank
            o_fw = _gla(cfg, u, low[:, :rank], odd_gate_b[i, 0], odd_gate_bias[i, 0], rev=False)
            og = _gla(cfg, u, low[:, rank:], odd_gate_b[i, 1], odd_gate_bias[i, 1], rev=True,
                      o_fw=o_fw, head_norm=odd_head_norm[i])
            h = _mm_res(cfg, [og], ow_out, (i,), h, 1.0)
        h = ffn(h, layer, 1, 2)
    return h.reshape(bsz, cfg.lp, d)[:, cfg.n_pad + cfg.n_meta:]


def kernel(x, meta_tokens, pre_norm, ffn_w_gate, ffn_w_up, ffn_w_down, even_w_in, even_q_norm,
           even_k_norm, even_w_out, odd_w_in, odd_gate_a, odd_gate_b, odd_gate_bias,
           odd_head_norm, odd_w_out):
    bsz, seq, d = x.shape
    cfg = Cfg(batch=bsz, seq=seq, d_model=d, d_ff=ffn_w_gate.shape[-1], depth=pre_norm.shape[0],
              n_meta=N_META, grid_w=GRID_W, attn_heads=ATTN_HEADS, attn_kv_heads=ATTN_KV_HEADS,
              head_dim=HEAD_DIM, q_block=Q_BLOCK, four_groups=FOURIER_GROUPS,
              four_dim=FOURIER_GROUP_DIM, gla_heads=GLA_HEADS, gla_rank=odd_gate_a.shape[-1],
              tm_res=1280, tm=640, tn=512, attn_tk=1024, attn_unroll=8, dft_tm=1408, dft_tn=1024, dft_tk=640,
              gla_chunk=128, gla_sub=4, gla_heads_per_step=8)
    return _forward(cfg, x, meta_tokens, pre_norm, ffn_w_gate, ffn_w_up, ffn_w_down, even_w_in,
                    even_q_norm, even_k_norm, even_w_out, odd_w_in, odd_gate_a, odd_gate_b,
                    odd_gate_bias, odd_head_norm, odd_w_out)
```

```python
import functools
import math
from typing import NamedTuple

import jax
import jax.numpy as jnp
from jax import lax
from jax.experimental import pallas as pl
from jax.experimental.pallas import tpu as pltpu

F32 = jnp.float32
BF16 = jnp.bfloat16

N_META = 16
GRID_W = 64
EPS = 1e-6
ATTN_HEADS = 16
ATTN_KV_HEADS = 4
HEAD_DIM = 128
Q_BLOCK = 128
ROPE_THETA = 10000.0
FOURIER_GROUPS = 16
FOURIER_GROUP_DIM = 128
GLA_HEADS = 8
GLA_GATE_TAU = 16.0

V7X_VMEM_BYTES = 64 * 1024 * 1024
VMEM_LIMIT_BYTES = V7X_VMEM_BYTES - 8 * 1024 * 1024
LANES = 128
SUBLANES = 8
NEG_BIG = -1e30


class Cfg(NamedTuple):
    batch: int
    seq: int
    d_model: int
    d_ff: int
    depth: int
    n_meta: int
    grid_w: int
    attn_heads: int
    attn_kv_heads: int
    head_dim: int
    q_block: int
    four_groups: int
    four_dim: int
    gla_heads: int
    gla_rank: int
    tm_res: int
    tm: int
    tn: int
    attn_tk: int
    attn_unroll: int
    dft_tm: int
    dft_tn: int
    dft_tk: int
    gla_chunk: int
    gla_sub: int
    gla_heads_per_step: int

    @property
    def length(self):
        return self.n_meta + self.seq

    @property
    def n_pad(self):
        return (-self.length) % self.q_block

    @property
    def lp(self):
        return self.length + self.n_pad

    @property
    def rows(self):
        return self.batch * self.lp

    @property
    def q_w(self):
        return self.attn_heads * self.head_dim

    @property
    def kv_w(self):
        return self.attn_kv_heads * self.head_dim

    @property
    def four_w(self):
        return self.four_groups * self.four_dim

    @property
    def gla_dk(self):
        return self.d_model // (2 * self.gla_heads)

    @property
    def gla_dv(self):
        return self.d_model // self.gla_heads


def _params(*sem):
    return pltpu.CompilerParams(dimension_semantics=sem, vmem_limit_bytes=VMEM_LIMIT_BYTES)


def _dot(a, b):
    return jnp.dot(a, b, preferred_element_type=F32)


def _dot_nt(a, b):
    return lax.dot_general(a, b, (((1,), (1,)), ((), ())), preferred_element_type=F32)


def _dot_tn(a, b):
    return lax.dot_general(a, b, (((0,), (0,)), ((), ())), preferred_element_type=F32)


NORM_ROWS = 16
NORM_UNROLL = 8


def _normalize_rows(h_ref, g_ref, xn_ref, eps):
    @pl.when(pl.program_id(1) == 0)
    def _():
        gain = g_ref[...]

        def body(r, carry):
            rows = pl.ds(pl.multiple_of(r * NORM_ROWS, NORM_ROWS), NORM_ROWS)
            x = h_ref[rows, :]
            ms = jnp.mean(x * x, axis=-1, keepdims=True)
            xn_ref[rows, :] = (x * lax.rsqrt(ms + eps) * gain).astype(xn_ref.dtype)
            return carry

        lax.fori_loop(0, h_ref.shape[0] // NORM_ROWS, body, 0, unroll=NORM_UNROLL)


def _norm_specs(tm, d):
    return [pl.BlockSpec((tm, d), lambda i, j: (i, 0)), pl.BlockSpec((1, d), lambda i, j: (0, 0))]


def _ffn_up_kernel(h_ref, g_ref, wg_ref, wu_ref, o_ref, xn_ref, *, eps):
    _normalize_rows(h_ref, g_ref, xn_ref, eps)
    x = xn_ref[...]
    g = _dot(x, wg_ref[...])
    u = _dot(x, wu_ref[...])
    o_ref[...] = (g * jax.nn.sigmoid(g) * u).astype(o_ref.dtype)


def _ffn_up(cfg, h, gain, wg, wu, layer, slot):
    m, d = h.shape
    f = wg.shape[-1]
    tm, tn = cfg.tm, cfg.tn
    assert tm % NORM_ROWS == 0
    wspec = pl.BlockSpec((None, None, d, tn), lambda i, j: (layer, slot, 0, j))
    return pl.pallas_call(
        functools.partial(_ffn_up_kernel, eps=EPS),
        out_shape=jax.ShapeDtypeStruct((m, f), BF16),
        grid=(m // tm, f // tn),
        in_specs=_norm_specs(tm, d) + [wspec, wspec],
        out_specs=pl.BlockSpec((tm, tn), lambda i, j: (i, j)),
        scratch_shapes=[pltpu.VMEM((tm, d), BF16)],
        compiler_params=_params("parallel", "arbitrary"),
        name="ffn_up",
    )(h, gain.reshape(1, d), wg, wu)


def _mm_res_kernel(*refs, n_lhs, scale):
    xs, ws = refs[:n_lhs], refs[n_lhs:2 * n_lhs]
    h_ref, o_ref = refs[2 * n_lhs], refs[2 * n_lhs + 1]
    acc = _dot(xs[0][...], ws[0][...])
    for x_ref, w_ref in zip(xs[1:], ws[1:]):
        acc = acc + _dot(x_ref[...], w_ref[...])
    o_ref[...] = h_ref[...] + scale * acc


def _mm_res(cfg, xs, w, w_lead, h, scale):
    m, d = h.shape
    kx = xs[0].shape[1]
    n_lhs = len(xs)
    tm, tn = cfg.tm_res, cfg.tn
    n_lead = len(w_lead)
    w_specs = [
        pl.BlockSpec((None,) * n_lead + (kx, tn), lambda i, j, p=p: tuple(w_lead) + (p, j))
        for p in range(n_lhs)
    ]
    return pl.pallas_call(
        functools.partial(_mm_res_kernel, n_lhs=n_lhs, scale=scale),
        out_shape=jax.ShapeDtypeStruct((m, d), F32),
        grid=(m // tm, d // tn),
        in_specs=[pl.BlockSpec((tm, kx), lambda i, j: (i, 0))] * n_lhs + w_specs
        + [pl.BlockSpec((tm, tn), lambda i, j: (i, j))],
        out_specs=pl.BlockSpec((tm, tn), lambda i, j: (i, j)),
        input_output_aliases={2 * n_lhs: 0},
        compiler_params=_params("parallel", "arbitrary"),
        name="mm_residual",
    )(*xs, *([w] * n_lhs), h)


def _even_inproj_kernel(h_ref, g_ref, w_ref, cos_ref, sin_ref, qg_ref, kg_ref, o_ref, qt_ref,
                        vt_ref, xn_ref, *, n_q_tiles, n_k_tiles, hd, eps, q_scale):
    _normalize_rows(h_ref, g_ref, xn_ref, eps)
    j = pl.program_id(1)
    acc = _dot(xn_ref[...], w_ref[...])
    heads_per_tile = acc.shape[1] // hd

    def normed_rotated(gain):
        c = cos_ref[...]
        s = sin_ref[...]
        lane = lax.broadcasted_iota(jnp.int32, c.shape, 1)
        first_half = (lane % (hd // 2)) < (hd // 4)
        for hh in range(heads_per_tile):
            xh = acc[:, hh * hd:(hh + 1) * hd]
            ms = jnp.mean(xh * xh, axis=-1, keepdims=True)
            y = xh * lax.rsqrt(ms + eps) * gain
            partner = jnp.where(first_half,
                                pltpu.roll(y, hd - hd // 4, axis=1),
                                pltpu.roll(y, hd // 4, axis=1))
            yield hh, y * c + partner * s

    @pl.when(j < n_q_tiles)
    def _():
        for hh, y in normed_rotated(qg_ref[...] * q_scale):
            o_ref[:, hh * hd:(hh + 1) * hd] = y.astype(o_ref.dtype)
            qt_ref[hh * hd:(hh + 1) * hd, :] = y.T.astype(qt_ref.dtype)

    @pl.when((j >= n_q_tiles) & (j < n_q_tiles + n_k_tiles))
    def _():
        for hh, y in normed_rotated(kg_ref[...]):
            o_ref[:, hh * hd:(hh + 1) * hd] = y.astype(o_ref.dtype)

    @pl.when(j >= n_q_tiles + n_k_tiles)
    def _():
        o_ref[...] = acc.astype(o_ref.dtype)

    @pl.when((j >= n_q_tiles + n_k_tiles) & (j < n_q_tiles + 2 * n_k_tiles))
    def _():
        vt_ref[...] = acc.T.astype(vt_ref.dtype)


def _even_inproj(cfg, h, gain, w, layer_i, cos_t, sin_t, q_gain, k_gain):
    m, d = h.shape
    n = w.shape[-1]
    tm, tn, hd = cfg.tm, cfg.tn, cfg.head_dim
    assert cfg.q_w % tn == 0 and cfg.kv_w % tn == 0 and tn % hd == 0 and tm % NORM_ROWS == 0
    n_q, n_k = cfg.q_w // tn, cfg.kv_w // tn
    kern = functools.partial(
        _even_inproj_kernel, n_q_tiles=n_q, n_k_tiles=n_k, hd=hd,
        eps=EPS, q_scale=hd ** -0.5 * math.log2(math.e))
    return pl.pallas_call(
        kern,
        out_shape=(jax.ShapeDtypeStruct((m, n), BF16),
                   jax.ShapeDtypeStruct((cfg.q_w, m), BF16),
                   jax.ShapeDtypeStruct((cfg.kv_w, m), BF16)),
        grid=(m // tm, n // tn),
        in_specs=_norm_specs(tm, d) + [
            pl.BlockSpec((None, d, tn), lambda i, j: (layer_i, 0, j)),
            pl.BlockSpec((tm, hd), lambda i, j: (i, 0)),
            pl.BlockSpec((tm, hd), lambda i, j: (i, 0)),
            pl.BlockSpec((1, hd), lambda i, j: (0, 0)),
            pl.BlockSpec((1, hd), lambda i, j: (0, 0))],
        out_specs=(
            pl.BlockSpec((tm, tn), lambda i, j: (i, j)),
            pl.BlockSpec((tn, tm), lambda i, j: (jnp.minimum(j, n_q - 1), i)),
            pl.BlockSpec((tn, tm), lambda i, j: (jnp.clip(j - n_q - n_k, 0, n_k - 1), i))),
        scratch_shapes=[pltpu.VMEM((tm, d), BF16)],
        compiler_params=_params("parallel", "arbitrary"),
        name="even_inproj",
    )(h, gain.reshape(1, d), w, cos_t, sin_t, q_gain.reshape(1, hd), k_gain.reshape(1, hd))


def _attn_kernel(qt_ref, k_ref, vt_ref, o_ref, acc_ref, *, groups, hpc, hd, head, sub, n_sub,
                 n_pad, unroll):
    tq = qt_ref.shape[1]
    n_chain = groups // hpc
    cw = hpc * tq
    q_t = [jnp.concatenate([qt_ref[(c * hpc + g) * hd:(c * hpc + g + 1) * hd, :]
                            for g in range(hpc)], axis=1) for c in range(n_chain)]

    def scores(start, size):
        k_blk = k_ref[pl.ds(start, size), :]
        return tuple(_dot(k_blk, q_t[c]) for c in range(n_chain))

    def consume(start, size, st, ml, masked):
        v_blk = vt_ref[:, pl.ds(start, size)]
        out = []
        for c in range(n_chain):
            m_prev, l_prev = ml[c]
            s = st[c]
            if masked:
                row = lax.broadcasted_iota(jnp.int32, s.shape, 0)
                s = jnp.where(row < n_pad, NEG_BIG, s)
            m_new = jnp.maximum(m_prev, jnp.max(s, axis=0, keepdims=True))
            alpha = jnp.exp2(m_prev - m_new)
            p = jnp.exp2(s - m_new)
            l_new = alpha * l_prev + jnp.sum(p, axis=0, keepdims=True)
            acc_ref[c] = alpha * acc_ref[c] + _dot(v_blk, p.astype(v_blk.dtype))
            out.append((m_new, l_new))
        return tuple(out)

    acc_ref[...] = jnp.zeros_like(acc_ref)
    ml = tuple((jnp.full((1, cw), NEG_BIG, F32), jnp.zeros((1, cw), F32))
               for _ in range(n_chain))
    def sub_start(j):
        return pl.multiple_of(head + j * sub, LANES)

    st = scores(0, head)
    st_next = scores(sub_start(0), sub)
    ml = consume(0, head, st, ml, True)

    def body(it, carry):
        st, ml = carry
        for u in range(unroll):
            j = it * unroll + u
            st_next = scores(sub_start(jnp.minimum(j + 1, n_sub - 1)), sub)
            ml = consume(sub_start(j), sub, st, ml, False)
            st = st_next
        return st, ml

    _, ml = lax.fori_loop(0, n_sub // unroll, body, (st_next, ml))
    for c in range(n_chain):
        o_t = acc_ref[c] / ml[c][1]
        for g in range(hpc):
            h0 = (c * hpc + g) * hd
            o_ref[:, h0:h0 + hd] = o_t[:, g * tq:(g + 1) * tq].T.astype(o_ref.dtype)


def _attention(cfg, u, qt, vt):
    m = u.shape[0]
    hd, kv, tq, lp = cfg.head_dim, cfg.attn_kv_heads, cfg.q_block, cfg.lp
    groups = cfg.attn_heads // kv
    hpc = min(groups, 2)
    n_q = lp // tq
    sub = cfg.attn_tk
    head = max(-(-cfg.n_pad // LANES), 1) * LANES
    n_sub = (lp - head) // sub
    assert (lp - head) % sub == 0 and n_sub % cfg.attn_unroll == 0 and sub % LANES == 0
    kern = functools.partial(_attn_kernel, groups=groups, hpc=hpc, hd=hd, head=head, sub=sub,
                             n_sub=n_sub, n_pad=cfg.n_pad, unroll=cfg.attn_unroll)
    k_col0 = cfg.q_w // hd
    return pl.pallas_call(
        kern,
        out_shape=jax.ShapeDtypeStruct((m, cfg.q_w), BF16),
        grid=(cfg.batch, kv, n_q),
        in_specs=[pl.BlockSpec((groups * hd, tq), lambda b, h, i: (h, b * n_q + i)),
                  pl.BlockSpec((lp, hd), lambda b, h, i: (b, k_col0 + h)),
                  pl.BlockSpec((hd, lp), lambda b, h, i: (h, b))],
        out_specs=pl.BlockSpec((tq, groups * hd), lambda b, h, i: (b * n_q + i, h)),
        scratch_shapes=[pltpu.VMEM((groups // hpc, hd, hpc * tq), F32)],
        compiler_params=_params("parallel", "parallel", "arbitrary"),
        name="gqa_attention",
    )(qt, u, vt)


def _chan_dft_kernel(f_ref, cs_ref, a_ref, b_ref, *, gd):
    cs = cs_ref[...]
    for g in range(f_ref.shape[1] // gd):
        ab = _dot(f_ref[:, g * gd:(g + 1) * gd], cs)
        a_ref[:, g * gd:(g + 1) * gd] = ab[:, :gd].astype(a_ref.dtype)
        b_ref[:, g * gd:(g + 1) * gd] = ab[:, gd:].astype(b_ref.dtype)


def _chan_dft(cfg, u, cs):
    m = u.shape[0]
    gd, fw = cfg.four_dim, cfg.four_w
    f0 = cfg.q_w + 2 * cfg.kv_w
    fb = math.gcd(math.gcd(f0, fw), 1024)
    tm = cfg.tm
    out = jax.ShapeDtypeStruct((m, fw), BF16)
    return pl.pallas_call(
        functools.partial(_chan_dft_kernel, gd=gd),
        out_shape=(out, out),
        grid=(m // tm, fw // fb),
        in_specs=[pl.BlockSpec((tm, fb), lambda i, j: (i, f0 // fb + j)),
                  pl.BlockSpec((gd, 2 * gd), lambda i, j: (0, 0))],
        out_specs=(pl.BlockSpec((tm, fb), lambda i, j: (i, j)),
                   pl.BlockSpec((tm, fb), lambda i, j: (i, j))),
        compiler_params=_params("parallel", "parallel"),
        name="fourier_channel_dft",
    )(u, cs)


def _seq_dft_kernel(wc_ref, ws_ref, a_ref, b_ref, d_ref, s_ref, accp_ref, accq_ref):
    k = pl.program_id(3)

    @pl.when(k == 0)
    def _():
        accp_ref[...] = jnp.zeros_like(accp_ref)
        accq_ref[...] = jnp.zeros_like(accq_ref)

    accp_ref[...] += _dot(wc_ref[...], a_ref[...])
    accq_ref[...] += _dot(ws_ref[...], b_ref[...])

    @pl.when(k == pl.num_programs(3) - 1)
    def _():
        p, q = accp_ref[...], accq_ref[...]
        d_ref[...] = (p - q).astype(d_ref.dtype)
        s_ref[...] = (p + q).astype(s_ref.dtype)


def _seq_dft_half(cfg, wc, ws, a, b):
    fw = a.shape[1]
    hh = wc.shape[0]
    lp = cfg.lp
    tm, tn, tk = cfg.dft_tm, cfg.dft_tn, cfg.dft_tk
    assert hh % tm == 0 and lp % tk == 0 and fw % tn == 0
    n_m, n_k = hh // tm, lp // tk
    out = jax.ShapeDtypeStruct((cfg.batch * hh, fw), BF16)
    return pl.pallas_call(
        _seq_dft_kernel,
        out_shape=(out, out),
        grid=(cfg.batch, fw // tn, n_m, n_k),
        in_specs=[pl.BlockSpec((tm, tk), lambda bb, n, i, k: (i, k)),
                  pl.BlockSpec((tm, tk), lambda bb, n, i, k: (i, k)),
                  pl.BlockSpec((tk, tn), lambda bb, n, i, k: (bb * n_k + k, n)),
                  pl.BlockSpec((tk, tn), lambda bb, n, i, k: (bb * n_k + k, n))],
        out_specs=(pl.BlockSpec((tm, tn), lambda bb, n, i, k: (bb * n_m + i, n)),
                   pl.BlockSpec((tm, tn), lambda bb, n, i, k: (bb * n_m + i, n))),
        scratch_shapes=[pltpu.VMEM((tm, tn), F32), pltpu.VMEM((tm, tn), F32)],
        compiler_params=_params("parallel", "parallel", "parallel", "arbitrary"),
        name="fourier_sequence_dft",
    )(wc, ws, a, b)


def _four_assemble_kernel(d_ref, s1_ref, s2_ref, j1_ref, j2_ref, o_ref, *, r_split):
    blk = pl.program_id(1)
    row = blk * LANES + lax.broadcasted_iota(jnp.int32, (LANES, 1), 0)
    rev = _dot(j1_ref[...], s1_ref[...]) + _dot(j2_ref[...], s2_ref[...])
    o_ref[...] = jnp.where(row < r_split, d_ref[...].astype(F32), rev).astype(o_ref.dtype)


def _four_assemble(cfg, dd, ss):
    fw = dd.shape[1]
    lp, n_pad = cfg.lp, cfg.n_pad
    n_blk = lp // LANES
    n_h = dd.shape[0] // cfg.batch // LANES
    r_split = n_pad + cfg.length // 2 + 1
    tn = fw
    s_idx = jnp.arange(LANES)[:, None]
    o_idx = jnp.arange(LANES)[None, :]
    j1 = ((s_idx <= n_pad) & (o_idx == n_pad - s_idx)).astype(BF16)
    j2 = ((s_idx > n_pad) & (o_idx == LANES + n_pad - s_idx)).astype(BF16)

    def half_blk(bb, idx):
        return bb * n_h + jnp.clip(idx, 0, n_h - 1)

    return pl.pallas_call(
        functools.partial(_four_assemble_kernel, r_split=r_split),
        out_shape=jax.ShapeDtypeStruct((cfg.batch * lp, fw), BF16),
        grid=(cfg.batch, n_blk, fw // tn),
        in_specs=[pl.BlockSpec((LANES, tn), lambda bb, i, n: (half_blk(bb, i), n)),
                  pl.BlockSpec((LANES, tn), lambda bb, i, n: (half_blk(bb, n_blk - i), n)),
                  pl.BlockSpec((LANES, tn), lambda bb, i, n: (half_blk(bb, n_blk - i - 1), n)),
                  pl.BlockSpec((LANES, LANES), lambda bb, i, n: (0, 0)),
                  pl.BlockSpec((LANES, LANES), lambda bb, i, n: (0, 0))],
        out_specs=pl.BlockSpec((LANES, tn), lambda bb, i, n: (bb * n_blk + i, n)),
        compiler_params=_params("parallel", "parallel", "parallel"),
        name="fourier_assemble",
    )(dd, ss, ss, j1, j2)


def _odd_inproj_kernel(h_ref, g_ref, w_ref, ga_ref, o_ref, low_ref, xn_ref, *, n_q_tiles,
                       q_scale, eps):
    _normalize_rows(h_ref, g_ref, xn_ref, eps)

    @pl.when(pl.program_id(1) == 0)
    def _():
        low_ref[...] = _dot(xn_ref[...], ga_ref[...])

    acc = _dot(xn_ref[...], w_ref[...])
    sc = jnp.where(pl.program_id(1) < n_q_tiles, q_scale, 1.0).astype(F32)
    o_ref[...] = (acc * sc).astype(o_ref.dtype)


def _odd_inproj(cfg, h, gain, w, layer_i, ga):
    m, d = h.shape
    n = w.shape[-1]
    r2 = ga.shape[-1]
    tm, tn = cfg.tm, cfg.tn
    assert tm % NORM_ROWS == 0
    kern = functools.partial(_odd_inproj_kernel, n_q_tiles=cfg.gla_heads * cfg.gla_dk // tn,
                             q_scale=cfg.gla_dk ** -0.5, eps=EPS)
    return pl.pallas_call(
        kern,
        out_shape=(jax.ShapeDtypeStruct((m, n), BF16), jax.ShapeDtypeStruct((m, r2), F32)),
        grid=(m // tm, n // tn),
        in_specs=_norm_specs(tm, d) + [
            pl.BlockSpec((None, d, tn), lambda i, j: (layer_i, 0, j)),
            pl.BlockSpec((None, d, r2), lambda i, j: (layer_i, 0, 0))],
        out_specs=(pl.BlockSpec((tm, tn), lambda i, j: (i, j)),
                   pl.BlockSpec((tm, r2), lambda i, j: (i, 0))),
        scratch_shapes=[pltpu.VMEM((tm, d), BF16)],
        compiler_params=_params("parallel", "arbitrary"),
        name="odd_inproj",
    )(h, gain.reshape(1, d), w, ga)


def _gla_kernel(*refs, rev, chunk, sub, halo, hpb, dk, dv, n_chunks, n_pad, tau, eps):
    if rev:
        (q_ref, k_ref, v_ref, low_ref, gb_ref, bias_ref, tri_ref, lmask_ref,
         ofw_ref, r_ref, hn_ref, o_ref, s_ref, cump_ref, kfp_ref) = refs
    else:
        (q_ref, k_ref, v_ref, low_ref, gb_ref, bias_ref, tri_ref, lmask_ref,
         o_ref, s_ref, cump_ref, kfp_ref) = refs
    c = pl.program_id(2)
    pos_chunk = (n_chunks - 1 - c) if rev else c

    @pl.when(c == 0)
    def _():
        s_ref[...] = jnp.zeros_like(s_ref)
        cump_ref[...] = jnp.zeros_like(cump_ref)
        kfp_ref[...] = jnp.zeros_like(kfp_ref)

    row = pos_chunk * chunk + lax.broadcasted_iota(jnp.int32, (chunk, 1), 0)
    valid = row >= n_pad
    col_minus_row = (lax.broadcasted_iota(jnp.int32, (chunk, chunk), 1)
                     - lax.broadcasted_iota(jnp.int32, (chunk, chunk), 0))
    row_in_sub = lax.broadcasted_iota(jnp.int32, (chunk, 1), 0) % sub
    low = low_ref[...]
    tri = tri_ref[...]

    def head(hx):
        ks = slice(hx * dk, (hx + 1) * dk)
        vs = slice(hx * dv, (hx + 1) * dv)
        z = jnp.dot(low, gb_ref[:, ks], preferred_element_type=F32,
                    precision=lax.Precision.HIGHEST) + bias_ref[:, ks]
        log_a = (jnp.minimum(z, 0.0) - jnp.log(1.0 + jnp.exp(-jnp.abs(z)))) * (1.0 / tau)
        log_a = jnp.where(valid, log_a, 0.0)
        kf = jnp.where(valid, k_ref[:, ks].astype(F32), 0.0)
        qf = q_ref[:, ks].astype(F32)
        v = v_ref[:, vs]

        g_hi = log_a.astype(BF16)
        g_lo = (log_a - g_hi.astype(F32)).astype(BF16)
        cum = _dot(tri, g_hi) + _dot(tri, g_lo)
        cum_end = cum[0:1] if rev else cum[chunk - 1:chunk]
        yield

        s_old = s_ref[hx]
        inter = _dot_nt((qf * jnp.exp(cum)).astype(BF16), s_old.astype(BF16))
        yield

        a_acc = jnp.zeros((chunk, chunk), F32)
        s_blk, level = chunk // 2, 0
        while s_blk >= sub:
            grp = 2 * s_blk
            ref_idx = s_blk if rev else s_blk - 1
            ref = jnp.broadcast_to(
                cum.reshape(chunk // grp, grp, dk)[:, ref_idx:ref_idx + 1, :],
                (chunk // grp, grp, dk)).reshape(chunk, dk)
            e = jnp.exp(-jnp.abs(cum - ref))
            a_lvl = _dot_nt((qf * e).astype(BF16), (kf * e).astype(BF16))
            a_acc = a_acc + a_lvl * lmask_ref[level]
            s_blk //= 2
            level += 1
            yield

        cump_ref[hx, halo:halo + chunk, :] = cum
        kfp_ref[hx, halo:halo + chunk, :] = kf
        a_acc = a_acc + jnp.where(col_minus_row == 0,
                                  jnp.sum(qf * kf, axis=-1, keepdims=True), 0.0)
        for delta in range(1, sub):
            off = halo + delta if rev else halo - delta
            k_sh = kfp_ref[hx, off:off + chunk, :]
            cum_sh = cump_ref[hx, off:off + chunk, :]
            col = jnp.sum(qf * k_sh * jnp.exp(cum - cum_sh), axis=-1, keepdims=True)
            same_sub = (row_in_sub + delta < sub) if rev else (row_in_sub >= delta)
            col = jnp.where(same_sub, col, 0.0)
            a_acc = a_acc + jnp.where(col_minus_row == (delta if rev else -delta), col, 0.0)
            yield

        o = inter + _dot(a_acc.astype(BF16), v)

        k_out = (kf * jnp.exp(cum_end - cum)).astype(BF16)
        s_ref[hx] = s_old * jnp.exp(cum_end) + _dot_tn(v, k_out)
        yield

        if rev:
            tot = ofw_ref[:, vs] + o
            ms = jnp.mean(tot * tot, axis=-1, keepdims=True)
            y = tot * lax.rsqrt(ms + eps) * hn_ref[...]
            r = r_ref[:, vs].astype(F32)
            o_ref[:, vs] = (y * r * jax.nn.sigmoid(r)).astype(o_ref.dtype)
        else:
            o_ref[:, vs] = o

    chains = [head(hx) for hx in range(hpb)]
    while chains:
        for g in list(chains):
            if next(g, "done") == "done":
                chains.remove(g)


def _gla_constants(cfg, rev):
    c, sub = cfg.gla_chunk, cfg.gla_sub
    i = jnp.arange(c)[:, None]
    j = jnp.arange(c)[None, :]
    tri = ((j >= i) if rev else (j <= i)).astype(BF16)
    masks = []
    s = c // 2
    while s >= sub:
        i_hi, j_hi = (i // s) % 2 == 1, (j // s) % 2 == 1
        roles = (~i_hi & j_hi) if rev else (i_hi & ~j_hi)
        masks.append(((i // (2 * s) == j // (2 * s)) & roles).astype(F32))
        s //= 2
    return tri, jnp.stack(masks)


def _gla(cfg, u, low, gate_b, gate_bias, rev, o_fw=None, head_norm=None):
    m = u.shape[0]
    hh, dk, dv, c = cfg.gla_heads, cfg.gla_dk, cfg.gla_dv, cfg.gla_chunk
    rank = cfg.gla_rank
    n_chunks = cfg.lp // c
    tri, lmask = _gla_constants(cfg, rev)
    n_lvl = lmask.shape[0]
    hpb = cfg.gla_heads_per_step
    kw, vw = hpb * dk, hpb * dv
    assert hh % hpb == 0 and (2 * hh * dk) % vw == 0
    k_col0 = hh // hpb
    v_col0 = 2 * hh * dk // vw
    r_col0 = v_col0 + hh // hpb

    def rowblk(b, h, ci):
        return b * n_chunks + ((n_chunks - 1 - ci) if rev else ci)

    in_specs = [
        pl.BlockSpec((c, kw), lambda b, h, ci: (rowblk(b, h, ci), h)),
        pl.BlockSpec((c, kw), lambda b, h, ci: (rowblk(b, h, ci), k_col0 + h)),
        pl.BlockSpec((c, vw), lambda b, h, ci: (rowblk(b, h, ci), v_col0 + h)),
        pl.BlockSpec((c, rank), lambda b, h, ci: (rowblk(b, h, ci), 0)),
        pl.BlockSpec((rank, kw), lambda b, h, ci: (0, h)),
        pl.BlockSpec((1, kw), lambda b, h, ci: (0, h)),
        pl.BlockSpec((c, c), lambda b, h, ci: (0, 0)),
        pl.BlockSpec((n_lvl, c, c), lambda b, h, ci: (0, 0, 0)),
    ]
    args = [u, u, u, low, gate_b, gate_bias.reshape(1, -1), tri, lmask]
    if rev:
        in_specs += [
            pl.BlockSpec((c, vw), lambda b, h, ci: (rowblk(b, h, ci), h)),
            pl.BlockSpec((c, vw), lambda b, h, ci: (rowblk(b, h, ci), r_col0 + h)),
            pl.BlockSpec((1, dv), lambda b, h, ci: (0, 0)),
        ]
        args += [o_fw, u, head_norm.reshape(1, dv)]
    halo = -(-cfg.gla_sub // SUBLANES) * SUBLANES
    kern = functools.partial(_gla_kernel, rev=rev, chunk=c, sub=cfg.gla_sub, halo=halo,
                             hpb=hpb, dk=dk, dv=dv, n_chunks=n_chunks, n_pad=cfg.n_pad,
                             tau=GLA_GATE_TAU, eps=EPS)
    return pl.pallas_call(
        kern,
        out_shape=jax.ShapeDtypeStruct((m, hh * dv), BF16 if rev else F32),
        grid=(cfg.batch, hh // hpb, n_chunks),
        in_specs=in_specs,
        out_specs=pl.BlockSpec((c, vw), lambda b, h, ci: (rowblk(b, h, ci), h)),
        scratch_shapes=[pltpu.VMEM((hpb, dv, dk), F32),
                        pltpu.VMEM((hpb, c + 2 * halo, dk), F32),
                        pltpu.VMEM((hpb, c + 2 * halo, dk), F32)],
        compiler_params=_params("parallel", "parallel", "arbitrary"),
        name="gla_reverse" if rev else "gla_forward",
    )(*args)


def _rope_tables(cfg):
    hd = cfg.head_dim
    n_freq = hd // 4
    t = jnp.arange(cfg.seq)
    rows = (t // cfg.grid_w).astype(F32)
    cols = (t % cfg.grid_w).astype(F32)
    inv_freq = jnp.power(ROPE_THETA, -jnp.arange(n_freq, dtype=F32) / n_freq)
    ang = jnp.concatenate([jnp.tile(rows[:, None] * inv_freq, (1, 2)),
                           jnp.tile(cols[:, None] * inv_freq, (1, 2))], axis=1)
    ang = jnp.concatenate([jnp.zeros((cfg.n_pad + cfg.n_meta, hd), F32), ang], axis=0)
    sign = jnp.where((jnp.arange(hd) % (hd // 2)) < n_freq, -1.0, 1.0).astype(F32)
    cos_t = jnp.tile(jnp.cos(ang), (cfg.batch, 1))
    sin_t = jnp.tile(jnp.sin(ang) * sign, (cfg.batch, 1))
    return cos_t, sin_t


def _dft_tables(cfg):
    ll, gd, lp, n_pad = cfg.length, cfg.four_dim, cfg.lp, cfg.n_pad
    scale = 1.0 / math.sqrt(ll * gd)
    assert lp % LANES == 0 and ll % 2 == 0 and cfg.dft_tm % LANES == 0
    r_split = n_pad + ll // 2 + 1
    hh = -(-r_split // cfg.dft_tm) * cfg.dft_tm
    n_hi = hh // LANES
    t = jnp.maximum(jnp.arange(lp, dtype=jnp.int32) - n_pad, 0)[None, :]
    k_lo = jnp.arange(LANES, dtype=jnp.int32)[:, None]
    k_hi = jnp.arange(n_hi, dtype=jnp.int32)[:, None] * LANES - n_pad
    ang_lo = (2.0 * math.pi / ll) * ((k_lo * t) % ll).astype(F32)
    ang_hi = (2.0 * math.pi / ll) * ((k_hi * t) % ll).astype(F32)
    c_lo, s_lo = jnp.cos(ang_lo)[None], jnp.sin(ang_lo)[None]
    c_hi, s_hi = (jnp.cos(ang_hi) * scale)[:, None], (jnp.sin(ang_hi) * scale)[:, None]
    rows = jnp.arange(hh)
    row_ok = ((rows >= n_pad) & (rows < r_split)).reshape(n_hi, LANES, 1)
    live = row_ok & (jnp.arange(lp) >= n_pad)[None, None, :]
    wc = jnp.where(live, c_hi * c_lo - s_hi * s_lo, 0.0).astype(BF16).reshape(hh, lp)
    ws = jnp.where(live, s_hi * c_lo + c_hi * s_lo, 0.0).astype(BF16).reshape(hh, lp)
    cc = jnp.arange(gd, dtype=jnp.int32)
    ang_c = (2.0 * math.pi / gd) * ((cc[:, None] * cc[None, :]) % gd).astype(F32)
    cs = jnp.concatenate([jnp.cos(ang_c), jnp.sin(ang_c)], axis=1).astype(BF16)
    return wc, ws, cs


def _forward(cfg, x, meta_tokens, pre_norm, ffn_w_gate, ffn_w_up, ffn_w_down, even_w_in,
             even_q_norm, even_k_norm, even_w_out, odd_w_in, odd_gate_a, odd_gate_b,
             odd_gate_bias, odd_head_norm, odd_w_out):
    bsz, d = cfg.batch, cfg.d_model
    meta = jnp.broadcast_to(meta_tokens.astype(F32)[None], (bsz, cfg.n_meta, d))
    h = jnp.concatenate([jnp.zeros((bsz, cfg.n_pad, d), F32), meta, x.astype(F32)], axis=1)
    h = h.reshape(cfg.rows, d)

    wg, wu, wd = ffn_w_gate.astype(BF16), ffn_w_up.astype(BF16), ffn_w_down.astype(BF16)
    ew_in, ew_out = even_w_in.astype(BF16), even_w_out.astype(BF16)
    ow_in, ow_out = odd_w_in.astype(BF16), odd_w_out.astype(BF16)
    ga = jnp.concatenate([odd_gate_a[:, 0], odd_gate_a[:, 1]], axis=-1).astype(BF16)

    cos_t, sin_t = _rope_tables(cfg)
    wc, ws, cs = _dft_tables(cfg)

    def ffn(h, layer, slot, norm_slot):
        a = _ffn_up(cfg, h, pre_norm[layer, norm_slot], wg, wu, layer, slot)
        return _mm_res(cfg, [a], wd, (layer, slot), h, 0.5)

    for layer in range(cfg.depth):
        i = layer // 2
        h = ffn(h, layer, 0, 0)
        mix_gain = pre_norm[layer, 1]
        if layer % 2 == 0:
            u, qt, vt = _even_inproj(cfg, h, mix_gain, ew_in, i, cos_t, sin_t, even_q_norm[i],
                                     even_k_norm[i])
            attn = _attention(cfg, u, qt, vt)
            fa, fb = _chan_dft(cfg, u, cs)
            four = _four_assemble(cfg, *_seq_dft_half(cfg, wc, ws, fa, fb))
            h = _mm_res(cfg, [attn, four], ew_out, (i,), h, 1.0)
        else:
            u, low = _odd_inproj(cfg, h, mix_gain, ow_in, i, ga)
            rank = cfg.gla_rank
            o_fw = _gla(cfg, u, low[:, :rank], odd_gate_b[i, 0], odd_gate_bias[i, 0], rev=False)
            og = _gla(cfg, u, low[:, rank:], odd_gate_b[i, 1], odd_gate_bias[i, 1], rev=True,
                      o_fw=o_fw, head_norm=odd_head_norm[i])
            h = _mm_res(cfg, [og], ow_out, (i,), h, 1.0)
        h = ffn(h, layer, 1, 2)
    return h.reshape(bsz, cfg.lp, d)[:, cfg.n_pad + cfg.n_meta:]


def kernel(x, meta_tokens, pre_norm, ffn_w_gate, ffn_w_up, ffn_w_down, even_w_in, even_q_norm,
           even_k_norm, even_w_out, odd_w_in, odd_gate_a, odd_gate_b, odd_gate_bias,
           odd_head_norm, odd_w_out):
    bsz, seq, d = x.shape
    cfg = Cfg(batch=bsz, seq=seq, d_model=d, d_ff=ffn_w_gate.shape[-1], depth=pre_norm.shape[0],
              n_meta=N_META, grid_w=GRID_W, attn_heads=ATTN_HEADS, attn_kv_heads=ATTN_KV_HEADS,
              head_dim=HEAD_DIM, q_block=Q_BLOCK, four_groups=FOURIER_GROUPS,
              four_dim=FOURIER_GROUP_DIM, gla_heads=GLA_HEADS, gla_rank=odd_gate_a.shape[-1],
              tm_res=1280, tm=640, tn=512, attn_tk=1024, attn_unroll=8, dft_tm=1408, dft_tn=512, dft_tk=1664,
              gla_chunk=128, gla_sub=4, gla_heads_per_step=8)
    return _forward(cfg, x, meta_tokens, pre_norm, ffn_w_gate, ffn_w_up, ffn_w_down, even_w_in,
                    even_q_norm, even_k_norm, even_w_out, odd_w_in, odd_gate_a, odd_gate_b,
                    odd_gate_bias, odd_head_norm, odd_w_out)
```

```python
import functools
import math
from typing import NamedTuple

import jax
import jax.numpy as jnp
from jax import lax
from jax.experimental import pallas as pl
from jax.experimental.pallas import tpu as pltpu

F32 = jnp.float32
BF16 = jnp.bfloat16

N_META = 16
GRID_W = 64
EPS = 1e-6
ATTN_HEADS = 16
ATTN_KV_HEADS = 4
HEAD_DIM = 128
Q_BLOCK = 128
ROPE_THETA = 10000.0
FOURIER_GROUPS = 16
FOURIER_GROUP_DIM = 128
GLA_HEADS = 8
GLA_GATE_TAU = 16.0

V7X_VMEM_BYTES = 64 * 1024 * 1024
VMEM_LIMIT_BYTES = V7X_VMEM_BYTES - 8 * 1024 * 1024
LANES = 128
SUBLANES = 8
NEG_BIG = -1e30


class Cfg(NamedTuple):
    batch: int
    seq: int
    d_model: int
    d_ff: int
    depth: int
    n_meta: int
    grid_w: int
    attn_heads: int
    attn_kv_heads: int
    head_dim: int
    q_block: int
    four_groups: int
    four_dim: int
    gla_heads: int
    gla_rank: int
    tm_res: int
    tm: int
    tn: int
    attn_tk: int
    attn_unroll: int
    dft_tm: int
    dft_tn: int
    dft_tk: int
    gla_chunk: int
    gla_sub: int
    gla_heads_per_step: int

    @property
    def length(self):
        return self.n_meta + self.seq

    @property
    def n_pad(self):
        return (-self.length) % self.q_block

    @property
    def lp(self):
        return self.length + self.n_pad

    @property
    def rows(self):
        return self.batch * self.lp

    @property
    def q_w(self):
        return self.attn_heads * self.head_dim

    @property
    def kv_w(self):
        return self.attn_kv_heads * self.head_dim

    @property
    def four_w(self):
        return self.four_groups * self.four_dim

    @property
    def gla_dk(self):
        return self.d_model // (2 * self.gla_heads)

    @property
    def gla_dv(self):
        return self.d_model // self.gla_heads


def _params(*sem):
    return pltpu.CompilerParams(dimension_semantics=sem, vmem_limit_bytes=VMEM_LIMIT_BYTES)


def _dot(a, b):
    return jnp.dot(a, b, preferred_element_type=F32)


def _dot_nt(a, b):
    return lax.dot_general(a, b, (((1,), (1,)), ((), ())), preferred_element_type=F32)


def _dot_tn(a, b):
    return lax.dot_general(a, b, (((0,), (0,)), ((), ())), preferred_element_type=F32)


def _row_scale(ss_ref, d, eps):
    return lax.rsqrt(ss_ref[:, 0:1] * (1.0 / d) + eps)


def _prep_rows_kernel(h_ref, g_ref, hg_ref, ss_ref):
    x = h_ref[...]
    hg_ref[...] = (x * g_ref[...]).astype(hg_ref.dtype)
    ss_ref[...] = jnp.broadcast_to(jnp.sum(x * x, axis=-1, keepdims=True), ss_ref.shape)


def _prep_rows(cfg, h, gain):
    m, d = h.shape
    tm = LANES
    return pl.pallas_call(
        _prep_rows_kernel,
        out_shape=(jax.ShapeDtypeStruct((m, d), BF16), jax.ShapeDtypeStruct((m, LANES), F32)),
        grid=(m // tm,),
        in_specs=[pl.BlockSpec((tm, d), lambda i: (i, 0)), pl.BlockSpec((1, d), lambda i: (0, 0))],
        out_specs=(pl.BlockSpec((tm, d), lambda i: (i, 0)),
                   pl.BlockSpec((tm, LANES), lambda i: (i, 0))),
        compiler_params=_params("parallel"),
        name="prep_rows",
    )(h, gain.reshape(1, d))


def _x_specs(tm, d):
    return [pl.BlockSpec((tm, d), lambda i, j: (i, 0)),
            pl.BlockSpec((tm, LANES), lambda i, j: (i, 0))]


def _ffn_up_kernel(hg_ref, ss_ref, wg_ref, wu_ref, o_ref, *, eps):
    x = hg_ref[...]
    rs = _row_scale(ss_ref, x.shape[1], eps)
    g = _dot(x, wg_ref[...]) * rs
    u = _dot(x, wu_ref[...]) * rs
    o_ref[...] = (g * jax.nn.sigmoid(g) * u).astype(o_ref.dtype)


def _ffn_up(cfg, hg, ss, wg, wu, layer, slot):
    m, d = hg.shape
    f = wg.shape[-1]
    tm, tn = cfg.tm, cfg.tn
    wspec = pl.BlockSpec((None, None, d, tn), lambda i, j: (layer, slot, 0, j))
    return pl.pallas_call(
        functools.partial(_ffn_up_kernel, eps=EPS),
        out_shape=jax.ShapeDtypeStruct((m, f), BF16),
        grid=(m // tm, f // tn),
        in_specs=_x_specs(tm, d) + [wspec, wspec],
        out_specs=pl.BlockSpec((tm, tn), lambda i, j: (i, j)),
        compiler_params=_params("parallel", "arbitrary"),
        name="ffn_up",
    )(hg, ss, wg, wu)


def _mm_res_kernel(*refs, n_lhs, scale, emit_next):
    xs, ws = refs[:n_lhs], refs[n_lhs:2 * n_lhs]
    if emit_next:
        h_ref, g_ref, o_ref, hg_ref, ss_ref = refs[2 * n_lhs:2 * n_lhs + 5]
    else:
        h_ref, o_ref = refs[2 * n_lhs:2 * n_lhs + 2]
    acc = _dot(xs[0][...], ws[0][...])
    for x_ref, w_ref in zip(xs[1:], ws[1:]):
        acc = acc + _dot(x_ref[...], w_ref[...])
    o = h_ref[...] + scale * acc
    o_ref[...] = o
    if emit_next:
        hg_ref[...] = (o * g_ref[...]).astype(hg_ref.dtype)
        part = jnp.broadcast_to(jnp.sum(o * o, axis=-1, keepdims=True), ss_ref.shape)
        j = pl.program_id(1)

        @pl.when(j == 0)
        def _():
            ss_ref[...] = part

        @pl.when(j > 0)
        def _():
            ss_ref[...] += part


def _mm_res(cfg, xs, w, w_lead, h, scale, next_gain=None):
    m, d = h.shape
    kx = xs[0].shape[1]
    n_lhs = len(xs)
    tm, tn = cfg.tm_res, cfg.tn
    n_lead = len(w_lead)
    emit_next = next_gain is not None
    w_specs = [
        pl.BlockSpec((None,) * n_lead + (kx, tn), lambda i, j, p=p: tuple(w_lead) + (p, j))
        for p in range(n_lhs)
    ]
    tile = pl.BlockSpec((tm, tn), lambda i, j: (i, j))
    in_specs = [pl.BlockSpec((tm, kx), lambda i, j: (i, 0))] * n_lhs + w_specs + [tile]
    args = [*xs, *([w] * n_lhs), h]
    out_shape = jax.ShapeDtypeStruct((m, d), F32)
    out_specs = tile
    if emit_next:
        in_specs.append(pl.BlockSpec((1, tn), lambda i, j: (0, j)))
        args.append(next_gain.reshape(1, d))
        out_shape = (out_shape, jax.ShapeDtypeStruct((m, d), BF16),
                     jax.ShapeDtypeStruct((m, LANES), F32))
        out_specs = (tile, tile, pl.BlockSpec((tm, LANES), lambda i, j: (i, 0)))
    return pl.pallas_call(
        functools.partial(_mm_res_kernel, n_lhs=n_lhs, scale=scale, emit_next=emit_next),
        out_shape=out_shape,
        grid=(m // tm, d // tn),
        in_specs=in_specs,
        out_specs=out_specs,
        input_output_aliases={2 * n_lhs: 0},
        compiler_params=_params("parallel", "arbitrary"),
        name="mm_residual",
    )(*args)


def _even_inproj_kernel(hg_ref, ss_ref, w_ref, cos_ref, sin_ref, qg_ref, kg_ref, o_ref, qt_ref,
                        vt_ref, *, n_q_tiles, n_k_tiles, hd, eps, q_scale):
    j = pl.program_id(1)
    x = hg_ref[...]
    acc = _dot(x, w_ref[...]) * _row_scale(ss_ref, x.shape[1], eps)
    heads_per_tile = acc.shape[1] // hd

    def normed_rotated(gain):
        c = cos_ref[...]
        s = sin_ref[...]
        lane = lax.broadcasted_iota(jnp.int32, c.shape, 1)
        first_half = (lane % (hd // 2)) < (hd // 4)
        for hh in range(heads_per_tile):
            xh = acc[:, hh * hd:(hh + 1) * hd]
            ms = jnp.mean(xh * xh, axis=-1, keepdims=True)
            y = xh * lax.rsqrt(ms + eps) * gain
            partner = jnp.where(first_half,
                                pltpu.roll(y, hd - hd // 4, axis=1),
                                pltpu.roll(y, hd // 4, axis=1))
            yield hh, y * c + partner * s

    @pl.when(j < n_q_tiles)
    def _():
        for hh, y in normed_rotated(qg_ref[...] * q_scale):
            o_ref[:, hh * hd:(hh + 1) * hd] = y.astype(o_ref.dtype)
            qt_ref[hh * hd:(hh + 1) * hd, :] = y.T.astype(qt_ref.dtype)

    @pl.when((j >= n_q_tiles) & (j < n_q_tiles + n_k_tiles))
    def _():
        for hh, y in normed_rotated(kg_ref[...]):
            o_ref[:, hh * hd:(hh + 1) * hd] = y.astype(o_ref.dtype)

    @pl.when(j >= n_q_tiles + n_k_tiles)
    def _():
        o_ref[...] = acc.astype(o_ref.dtype)

    @pl.when((j >= n_q_tiles + n_k_tiles) & (j < n_q_tiles + 2 * n_k_tiles))
    def _():
        vt_ref[...] = acc.T.astype(vt_ref.dtype)


def _even_inproj(cfg, hg, ss, w, layer_i, cos_t, sin_t, q_gain, k_gain):
    m, d = hg.shape
    n = w.shape[-1]
    tm, tn, hd = cfg.tm, cfg.tn, cfg.head_dim
    assert cfg.q_w % tn == 0 and cfg.kv_w % tn == 0 and tn % hd == 0
    n_q, n_k = cfg.q_w // tn, cfg.kv_w // tn
    kern = functools.partial(
        _even_inproj_kernel, n_q_tiles=n_q, n_k_tiles=n_k, hd=hd,
        eps=EPS, q_scale=hd ** -0.5 * math.log2(math.e))
    return pl.pallas_call(
        kern,
        out_shape=(jax.ShapeDtypeStruct((m, n), BF16),
                   jax.ShapeDtypeStruct((cfg.q_w, m), BF16),
                   jax.ShapeDtypeStruct((cfg.kv_w, m), BF16)),
        grid=(m // tm, n // tn),
        in_specs=_x_specs(tm, d) + [
            pl.BlockSpec((None, d, tn), lambda i, j: (layer_i, 0, j)),
            pl.BlockSpec((tm, hd), lambda i, j: (i, 0)),
            pl.BlockSpec((tm, hd), lambda i, j: (i, 0)),
            pl.BlockSpec((1, hd), lambda i, j: (0, 0)),
            pl.BlockSpec((1, hd), lambda i, j: (0, 0))],
        out_specs=(
            pl.BlockSpec((tm, tn), lambda i, j: (i, j)),
            pl.BlockSpec((tn, tm), lambda i, j: (jnp.minimum(j, n_q - 1), i)),
            pl.BlockSpec((tn, tm), lambda i, j: (jnp.clip(j - n_q - n_k, 0, n_k - 1), i))),
        compiler_params=_params("parallel", "arbitrary"),
        name="even_inproj",
    )(hg, ss, w, cos_t, sin_t, q_gain.reshape(1, hd), k_gain.reshape(1, hd))


def _attn_kernel(qt_ref, k_ref, vt_ref, o_ref, acc_ref, *, groups, hpc, hd, head, sub, n_sub,
                 n_pad, unroll):
    tq = qt_ref.shape[1]
    n_chain = groups // hpc
    cw = hpc * tq
    q_t = [jnp.concatenate([qt_ref[(c * hpc + g) * hd:(c * hpc + g + 1) * hd, :]
                            for g in range(hpc)], axis=1) for c in range(n_chain)]

    def scores(start, size):
        k_blk = k_ref[pl.ds(start, size), :]
        return tuple(_dot(k_blk, q_t[c]) for c in range(n_chain))

    def consume(start, size, st, ml, masked):
        v_blk = vt_ref[:, pl.ds(start, size)]
        out = []
        for c in range(n_chain):
            m_prev, l_prev = ml[c]
            s = st[c]
            if masked:
                row = lax.broadcasted_iota(jnp.int32, s.shape, 0)
                s = jnp.where(row < n_pad, NEG_BIG, s)
            m_new = jnp.maximum(m_prev, jnp.max(s, axis=0, keepdims=True))
            alpha = jnp.exp2(m_prev - m_new)
            p = jnp.exp2(s - m_new)
            l_new = alpha * l_prev + jnp.sum(p, axis=0, keepdims=True)
            acc_ref[c] = alpha * acc_ref[c] + _dot(v_blk, p.astype(v_blk.dtype))
            out.append((m_new, l_new))
        return tuple(out)

    acc_ref[...] = jnp.zeros_like(acc_ref)
    ml = tuple((jnp.full((1, cw), NEG_BIG, F32), jnp.zeros((1, cw), F32))
               for _ in range(n_chain))
    def sub_start(j):
        return pl.multiple_of(head + j * sub, LANES)

    st = scores(0, head)
    st_next = scores(sub_start(0), sub)
    ml = consume(0, head, st, ml, True)

    def body(it, carry):
        st, ml = carry
        for u in range(unroll):
            j = it * unroll + u
            st_next = scores(sub_start(jnp.minimum(j + 1, n_sub - 1)), sub)
            ml = consume(sub_start(j), sub, st, ml, False)
            st = st_next
        return st, ml

    _, ml = lax.fori_loop(0, n_sub // unroll, body, (st_next, ml))
    for c in range(n_chain):
        o_t = acc_ref[c] / ml[c][1]
        for g in range(hpc):
            h0 = (c * hpc + g) * hd
            o_ref[:, h0:h0 + hd] = o_t[:, g * tq:(g + 1) * tq].T.astype(o_ref.dtype)


def _attention(cfg, u, qt, vt):
    m = u.shape[0]
    hd, kv, tq, lp = cfg.head_dim, cfg.attn_kv_heads, cfg.q_block, cfg.lp
    groups = cfg.attn_heads // kv
    hpc = min(groups, 2)
    n_q = lp // tq
    sub = cfg.attn_tk
    head = max(-(-cfg.n_pad // LANES), 1) * LANES
    n_sub = (lp - head) // sub
    assert (lp - head) % sub == 0 and n_sub % cfg.attn_unroll == 0 and sub % LANES == 0
    kern = functools.partial(_attn_kernel, groups=groups, hpc=hpc, hd=hd, head=head, sub=sub,
                             n_sub=n_sub, n_pad=cfg.n_pad, unroll=cfg.attn_unroll)
    k_col0 = cfg.q_w // hd
    return pl.pallas_call(
        kern,
        out_shape=jax.ShapeDtypeStruct((m, cfg.q_w), BF16),
        grid=(cfg.batch, kv, n_q),
        in_specs=[pl.BlockSpec((groups * hd, tq), lambda b, h, i: (h, b * n_q + i)),
                  pl.BlockSpec((lp, hd), lambda b, h, i: (b, k_col0 + h)),
                  pl.BlockSpec((hd, lp), lambda b, h, i: (h, b))],
        out_specs=pl.BlockSpec((tq, groups * hd), lambda b, h, i: (b * n_q + i, h)),
        scratch_shapes=[pltpu.VMEM((groups // hpc, hd, hpc * tq), F32)],
        compiler_params=_params("parallel", "parallel", "arbitrary"),
        name="gqa_attention",
    )(qt, u, vt)


def _chan_dft_kernel(f_ref, cs_ref, a_ref, b_ref, *, gd):
    cs = cs_ref[...]
    for g in range(f_ref.shape[1] // gd):
        ab = _dot(f_ref[:, g * gd:(g + 1) * gd], cs)
        a_ref[:, g * gd:(g + 1) * gd] = ab[:, :gd].astype(a_ref.dtype)
        b_ref[:, g * gd:(g + 1) * gd] = ab[:, gd:].astype(b_ref.dtype)


def _chan_dft(cfg, u, cs):
    m = u.shape[0]
    gd, fw = cfg.four_dim, cfg.four_w
    f0 = cfg.q_w + 2 * cfg.kv_w
    fb = math.gcd(math.gcd(f0, fw), 1024)
    tm = cfg.tm
    out = jax.ShapeDtypeStruct((m, fw), BF16)
    return pl.pallas_call(
        functools.partial(_chan_dft_kernel, gd=gd),
        out_shape=(out, out),
        grid=(m // tm, fw // fb),
        in_specs=[pl.BlockSpec((tm, fb), lambda i, j: (i, f0 // fb + j)),
                  pl.BlockSpec((gd, 2 * gd), lambda i, j: (0, 0))],
        out_specs=(pl.BlockSpec((tm, fb), lambda i, j: (i, j)),
                   pl.BlockSpec((tm, fb), lambda i, j: (i, j))),
        compiler_params=_params("parallel", "parallel"),
        name="fourier_channel_dft",
    )(u, cs)


def _seq_dft_kernel(wc_ref, ws_ref, a_ref, b_ref, d_ref, s_ref, accp_ref, accq_ref):
    k = pl.program_id(3)

    @pl.when(k == 0)
    def _():
        accp_ref[...] = jnp.zeros_like(accp_ref)
        accq_ref[...] = jnp.zeros_like(accq_ref)

    accp_ref[...] += _dot(wc_ref[...], a_ref[...])
    accq_ref[...] += _dot(ws_ref[...], b_ref[...])

    @pl.when(k == pl.num_programs(3) - 1)
    def _():
        p, q = accp_ref[...], accq_ref[...]
        d_ref[...] = (p - q).astype(d_ref.dtype)
        s_ref[...] = (p + q).astype(s_ref.dtype)


def _seq_dft_half(cfg, wc, ws, a, b):
    fw = a.shape[1]
    hh = wc.shape[0]
    lp = cfg.lp
    tm, tn, tk = cfg.dft_tm, cfg.dft_tn, cfg.dft_tk
    assert hh % tm == 0 and lp % tk == 0 and fw % tn == 0
    n_m, n_k = hh // tm, lp // tk
    out = jax.ShapeDtypeStruct((cfg.batch * hh, fw), BF16)
    return pl.pallas_call(
        _seq_dft_kernel,
        out_shape=(out, out),
        grid=(cfg.batch, fw // tn, n_m, n_k),
        in_specs=[pl.BlockSpec((tm, tk), lambda bb, n, i, k: (i, k)),
                  pl.BlockSpec((tm, tk), lambda bb, n, i, k: (i, k)),
                  pl.BlockSpec((tk, tn), lambda bb, n, i, k: (bb * n_k + k, n)),
                  pl.BlockSpec((tk, tn), lambda bb, n, i, k: (bb * n_k + k, n))],
        out_specs=(pl.BlockSpec((tm, tn), lambda bb, n, i, k: (bb * n_m + i, n)),
                   pl.BlockSpec((tm, tn), lambda bb, n, i, k: (bb * n_m + i, n))),
        scratch_shapes=[pltpu.VMEM((tm, tn), F32), pltpu.VMEM((tm, tn), F32)],
        compiler_params=_params("parallel", "parallel", "parallel", "arbitrary"),
        name="fourier_sequence_dft",
    )(wc, ws, a, b)


def _four_assemble_kernel(d_ref, s1_ref, s2_ref, j1_ref, j2_ref, o_ref, *, r_split):
    blk = pl.program_id(1)
    row = blk * LANES + lax.broadcasted_iota(jnp.int32, (LANES, 1), 0)
    rev = _dot(j1_ref[...], s1_ref[...]) + _dot(j2_ref[...], s2_ref[...])
    o_ref[...] = jnp.where(row < r_split, d_ref[...].astype(F32), rev).astype(o_ref.dtype)


def _four_assemble(cfg, dd, ss):
    fw = dd.shape[1]
    lp, n_pad = cfg.lp, cfg.n_pad
    n_blk = lp // LANES
    n_h = dd.shape[0] // cfg.batch // LANES
    r_split = n_pad + cfg.length // 2 + 1
    tn = fw
    s_idx = jnp.arange(LANES)[:, None]
    o_idx = jnp.arange(LANES)[None, :]
    j1 = ((s_idx <= n_pad) & (o_idx == n_pad - s_idx)).astype(BF16)
    j2 = ((s_idx > n_pad) & (o_idx == LANES + n_pad - s_idx)).astype(BF16)

    def half_blk(bb, idx):
        return bb * n_h + jnp.clip(idx, 0, n_h - 1)

    return pl.pallas_call(
        functools.partial(_four_assemble_kernel, r_split=r_split),
        out_shape=jax.ShapeDtypeStruct((cfg.batch * lp, fw), BF16),
        grid=(cfg.batch, n_blk, fw // tn),
        in_specs=[pl.BlockSpec((LANES, tn), lambda bb, i, n: (half_blk(bb, i), n)),
                  pl.BlockSpec((LANES, tn), lambda bb, i, n: (half_blk(bb, n_blk - i), n)),
                  pl.BlockSpec((LANES, tn), lambda bb, i, n: (half_blk(bb, n_blk - i - 1), n)),
                  pl.BlockSpec((LANES, LANES), lambda bb, i, n: (0, 0)),
                  pl.BlockSpec((LANES, LANES), lambda bb, i, n: (0, 0))],
        out_specs=pl.BlockSpec((LANES, tn), lambda bb, i, n: (bb * n_blk + i, n)),
        compiler_params=_params("parallel", "parallel", "parallel"),
        name="fourier_assemble",
    )(dd, ss, ss, j1, j2)


def _odd_inproj_kernel(hg_ref, ss_ref, w_ref, ga_ref, o_ref, low_ref, *, n_q_tiles, q_scale,
                       eps):
    rs = _row_scale(ss_ref, hg_ref.shape[1], eps)

    @pl.when(pl.program_id(1) == 0)
    def _():
        low_ref[...] = _dot(hg_ref[...], ga_ref[...]) * rs

    acc = _dot(hg_ref[...], w_ref[...])
    sc = jnp.where(pl.program_id(1) < n_q_tiles, q_scale, 1.0).astype(F32)
    o_ref[...] = (acc * (rs * sc)).astype(o_ref.dtype)


def _odd_inproj(cfg, hg, ss, w, layer_i, ga):
    m, d = hg.shape
    n = w.shape[-1]
    r2 = ga.shape[-1]
    tm, tn = cfg.tm, cfg.tn
    kern = functools.partial(_odd_inproj_kernel, n_q_tiles=cfg.gla_heads * cfg.gla_dk // tn,
                             q_scale=cfg.gla_dk ** -0.5, eps=EPS)
    return pl.pallas_call(
        kern,
        out_shape=(jax.ShapeDtypeStruct((m, n), BF16), jax.ShapeDtypeStruct((m, r2), F32)),
        grid=(m // tm, n // tn),
        in_specs=_x_specs(tm, d) + [
            pl.BlockSpec((None, d, tn), lambda i, j: (layer_i, 0, j)),
            pl.BlockSpec((None, d, r2), lambda i, j: (layer_i, 0, 0))],
        out_specs=(pl.BlockSpec((tm, tn), lambda i, j: (i, j)),
                   pl.BlockSpec((tm, r2), lambda i, j: (i, 0))),
        compiler_params=_params("parallel", "arbitrary"),
        name="odd_inproj",
    )(hg, ss, w, ga)


def _gla_kernel(*refs, rev, chunk, sub, halo, hpb, dk, dv, n_chunks, n_pad, tau, eps):
    if rev:
        (q_ref, k_ref, v_ref, low_ref, gb_ref, bias_ref, tri_ref, lmask_ref,
         ofw_ref, r_ref, hn_ref, o_ref, s_ref, cump_ref, kfp_ref) = refs
    else:
        (q_ref, k_ref, v_ref, low_ref, gb_ref, bias_ref, tri_ref, lmask_ref,
         o_ref, s_ref, cump_ref, kfp_ref) = refs
    c = pl.program_id(2)
    pos_chunk = (n_chunks - 1 - c) if rev else c

    @pl.when(c == 0)
    def _():
        s_ref[...] = jnp.zeros_like(s_ref)
        cump_ref[...] = jnp.zeros_like(cump_ref)
        kfp_ref[...] = jnp.zeros_like(kfp_ref)

    row = pos_chunk * chunk + lax.broadcasted_iota(jnp.int32, (chunk, 1), 0)
    valid = row >= n_pad
    col_minus_row = (lax.broadcasted_iota(jnp.int32, (chunk, chunk), 1)
                     - lax.broadcasted_iota(jnp.int32, (chunk, chunk), 0))
    row_in_sub = lax.broadcasted_iota(jnp.int32, (chunk, 1), 0) % sub
    low = low_ref[...]
    tri = tri_ref[...]

    def head(hx):
        ks = slice(hx * dk, (hx + 1) * dk)
        vs = slice(hx * dv, (hx + 1) * dv)
        z = jnp.dot(low, gb_ref[:, ks], preferred_element_type=F32,
                    precision=lax.Precision.HIGHEST) + bias_ref[:, ks]
        log_a = (jnp.minimum(z, 0.0) - jnp.log(1.0 + jnp.exp(-jnp.abs(z)))) * (1.0 / tau)
        log_a = jnp.where(valid, log_a, 0.0)
        kf = jnp.where(valid, k_ref[:, ks].astype(F32), 0.0)
        qf = q_ref[:, ks].astype(F32)
        v = v_ref[:, vs]

        g_hi = log_a.astype(BF16)
        g_lo = (log_a - g_hi.astype(F32)).astype(BF16)
        cum = _dot(tri, g_hi) + _dot(tri, g_lo)
        cum_end = cum[0:1] if rev else cum[chunk - 1:chunk]
        yield

        s_old = s_ref[hx]
        inter = _dot_nt((qf * jnp.exp(cum)).astype(BF16), s_old.astype(BF16))
        yield

        a_acc = jnp.zeros((chunk, chunk), F32)
        s_blk, level = chunk // 2, 0
        while s_blk >= sub:
            grp = 2 * s_blk
            ref_idx = s_blk if rev else s_blk - 1
            ref = jnp.broadcast_to(
                cum.reshape(chunk // grp, grp, dk)[:, ref_idx:ref_idx + 1, :],
                (chunk // grp, grp, dk)).reshape(chunk, dk)
            e = jnp.exp(-jnp.abs(cum - ref))
            a_lvl = _dot_nt((qf * e).astype(BF16), (kf * e).astype(BF16))
            a_acc = a_acc + a_lvl * lmask_ref[level]
            s_blk //= 2
            level += 1
            yield

        cump_ref[hx, halo:halo + chunk, :] = cum
        kfp_ref[hx, halo:halo + chunk, :] = kf
        a_acc = a_acc + jnp.where(col_minus_row == 0,
                                  jnp.sum(qf * kf, axis=-1, keepdims=True), 0.0)
        for delta in range(1, sub):
            off = halo + delta if rev else halo - delta
            k_sh = kfp_ref[hx, off:off + chunk, :]
            cum_sh = cump_ref[hx, off:off + chunk, :]
            col = jnp.sum(qf * k_sh * jnp.exp(cum - cum_sh), axis=-1, keepdims=True)
            same_sub = (row_in_sub + delta < sub) if rev else (row_in_sub >= delta)
            col = jnp.where(same_sub, col, 0.0)
            a_acc = a_acc + jnp.where(col_minus_row == (delta if rev else -delta), col, 0.0)
            yield

        o = inter + _dot(a_acc.astype(BF16), v)

        k_out = (kf * jnp.exp(cum_end - cum)).astype(BF16)
        s_ref[hx] = s_old * jnp.exp(cum_end) + _dot_tn(v, k_out)
        yield

        if rev:
            tot = ofw_ref[:, vs] + o
            ms = jnp.mean(tot * tot, axis=-1, keepdims=True)
            y = tot * lax.rsqrt(ms + eps) * hn_ref[...]
            r = r_ref[:, vs].astype(F32)
            o_ref[:, vs] = (y * r * jax.nn.sigmoid(r)).astype(o_ref.dtype)
        else:
            o_ref[:, vs] = o

    chains = [head(hx) for hx in range(hpb)]
    while chains:
        for g in list(chains):
            if next(g, "done") == "done":
                chains.remove(g)


def _gla_constants(cfg, rev):
    c, sub = cfg.gla_chunk, cfg.gla_sub
    i = jnp.arange(c)[:, None]
    j = jnp.arange(c)[None, :]
    tri = ((j >= i) if rev else (j <= i)).astype(BF16)
    masks = []
    s = c // 2
    while s >= sub:
        i_hi, j_hi = (i // s) % 2 == 1, (j // s) % 2 == 1
        roles = (~i_hi & j_hi) if rev else (i_hi & ~j_hi)
        masks.append(((i // (2 * s) == j // (2 * s)) & roles).astype(F32))
        s //= 2
    return tri, jnp.stack(masks)


def _gla(cfg, u, low, gate_b, gate_bias, rev, o_fw=None, head_norm=None):
    m = u.shape[0]
    hh, dk, dv, c = cfg.gla_heads, cfg.gla_dk, cfg.gla_dv, cfg.gla_chunk
    rank = cfg.gla_rank
    n_chunks = cfg.lp // c
    tri, lmask = _gla_constants(cfg, rev)
    n_lvl = lmask.shape[0]
    hpb = cfg.gla_heads_per_step
    kw, vw = hpb * dk, hpb * dv
    assert hh % hpb == 0 and (2 * hh * dk) % vw == 0
    k_col0 = hh // hpb
    v_col0 = 2 * hh * dk // vw
    r_col0 = v_col0 + hh // hpb

    def rowblk(b, h, ci):
        return b * n_chunks + ((n_chunks - 1 - ci) if rev else ci)

    in_specs = [
        pl.BlockSpec((c, kw), lambda b, h, ci: (rowblk(b, h, ci), h)),
        pl.BlockSpec((c, kw), lambda b, h, ci: (rowblk(b, h, ci), k_col0 + h)),
        pl.BlockSpec((c, vw), lambda b, h, ci: (rowblk(b, h, ci), v_col0 + h)),
        pl.BlockSpec((c, rank), lambda b, h, ci: (rowblk(b, h, ci), 0)),
        pl.BlockSpec((rank, kw), lambda b, h, ci: (0, h)),
        pl.BlockSpec((1, kw), lambda b, h, ci: (0, h)),
        pl.BlockSpec((c, c), lambda b, h, ci: (0, 0)),
        pl.BlockSpec((n_lvl, c, c), lambda b, h, ci: (0, 0, 0)),
    ]
    args = [u, u, u, low, gate_b, gate_bias.reshape(1, -1), tri, lmask]
    if rev:
        in_specs += [
            pl.BlockSpec((c, vw), lambda b, h, ci: (rowblk(b, h, ci), h)),
            pl.BlockSpec((c, vw), lambda b, h, ci: (rowblk(b, h, ci), r_col0 + h)),
            pl.BlockSpec((1, dv), lambda b, h, ci: (0, 0)),
        ]
        args += [o_fw, u, head_norm.reshape(1, dv)]
    halo = -(-cfg.gla_sub // SUBLANES) * SUBLANES
    kern = functools.partial(_gla_kernel, rev=rev, chunk=c, sub=cfg.gla_sub, halo=halo,
                             hpb=hpb, dk=dk, dv=dv, n_chunks=n_chunks, n_pad=cfg.n_pad,
                             tau=GLA_GATE_TAU, eps=EPS)
    return pl.pallas_call(
        kern,
        out_shape=jax.ShapeDtypeStruct((m, hh * dv), BF16 if rev else F32),
        grid=(cfg.batch, hh // hpb, n_chunks),
        in_specs=in_specs,
        out_specs=pl.BlockSpec((c, vw), lambda b, h, ci: (rowblk(b, h, ci), h)),
        scratch_shapes=[pltpu.VMEM((hpb, dv, dk), F32),
                        pltpu.VMEM((hpb, c + 2 * halo, dk), F32),
                        pltpu.VMEM((hpb, c + 2 * halo, dk), F32)],
        compiler_params=_params("parallel", "parallel", "arbitrary"),
        name="gla_reverse" if rev else "gla_forward",
    )(*args)


def _rope_tables(cfg):
    hd = cfg.head_dim
    n_freq = hd // 4
    t = jnp.arange(cfg.seq)
    rows = (t // cfg.grid_w).astype(F32)
    cols = (t % cfg.grid_w).astype(F32)
    inv_freq = jnp.power(ROPE_THETA, -jnp.arange(n_freq, dtype=F32) / n_freq)
    ang = jnp.concatenate([jnp.tile(rows[:, None] * inv_freq, (1, 2)),
                           jnp.tile(cols[:, None] * inv_freq, (1, 2))], axis=1)
    ang = jnp.concatenate([jnp.zeros((cfg.n_pad + cfg.n_meta, hd), F32), ang], axis=0)
    sign = jnp.where((jnp.arange(hd) % (hd // 2)) < n_freq, -1.0, 1.0).astype(F32)
    cos_t = jnp.tile(jnp.cos(ang), (cfg.batch, 1))
    sin_t = jnp.tile(jnp.sin(ang) * sign, (cfg.batch, 1))
    return cos_t, sin_t


def _dft_tables(cfg):
    ll, gd, lp, n_pad = cfg.length, cfg.four_dim, cfg.lp, cfg.n_pad
    scale = 1.0 / math.sqrt(ll * gd)
    assert lp % LANES == 0 and ll % 2 == 0 and cfg.dft_tm % LANES == 0
    r_split = n_pad + ll // 2 + 1
    hh = -(-r_split // cfg.dft_tm) * cfg.dft_tm
    n_hi = hh // LANES
    t = jnp.maximum(jnp.arange(lp, dtype=jnp.int32) - n_pad, 0)[None, :]
    k_lo = jnp.arange(LANES, dtype=jnp.int32)[:, None]
    k_hi = jnp.arange(n_hi, dtype=jnp.int32)[:, None] * LANES - n_pad
    ang_lo = (2.0 * math.pi / ll) * ((k_lo * t) % ll).astype(F32)
    ang_hi = (2.0 * math.pi / ll) * ((k_hi * t) % ll).astype(F32)
    c_lo, s_lo = jnp.cos(ang_lo)[None], jnp.sin(ang_lo)[None]
    c_hi, s_hi = (jnp.cos(ang_hi) * scale)[:, None], (jnp.sin(ang_hi) * scale)[:, None]
    rows = jnp.arange(hh)
    row_ok = ((rows >= n_pad) & (rows < r_split)).reshape(n_hi, LANES, 1)
    live = row_ok & (jnp.arange(lp) >= n_pad)[None, None, :]
    wc = jnp.where(live, c_hi * c_lo - s_hi * s_lo, 0.0).astype(BF16).reshape(hh, lp)
    ws = jnp.where(live, s_hi * c_lo + c_hi * s_lo, 0.0).astype(BF16).reshape(hh, lp)
    cc = jnp.arange(gd, dtype=jnp.int32)
    ang_c = (2.0 * math.pi / gd) * ((cc[:, None] * cc[None, :]) % gd).astype(F32)
    cs = jnp.concatenate([jnp.cos(ang_c), jnp.sin(ang_c)], axis=1).astype(BF16)
    return wc, ws, cs


def _forward(cfg, x, meta_tokens, pre_norm, ffn_w_gate, ffn_w_up, ffn_w_down, even_w_in,
             even_q_norm, even_k_norm, even_w_out, odd_w_in, odd_gate_a, odd_gate_b,
             odd_gate_bias, odd_head_norm, odd_w_out):
    bsz, d = cfg.batch, cfg.d_model
    meta = jnp.broadcast_to(meta_tokens.astype(F32)[None], (bsz, cfg.n_meta, d))
    h = jnp.concatenate([jnp.zeros((bsz, cfg.n_pad, d), F32), meta, x.astype(F32)], axis=1)
    h = h.reshape(cfg.rows, d)

    wg, wu, wd = ffn_w_gate.astype(BF16), ffn_w_up.astype(BF16), ffn_w_down.astype(BF16)
    ew_in, ew_out = even_w_in.astype(BF16), even_w_out.astype(BF16)
    ow_in, ow_out = odd_w_in.astype(BF16), odd_w_out.astype(BF16)
    ga = jnp.concatenate([odd_gate_a[:, 0], odd_gate_a[:, 1]], axis=-1).astype(BF16)

    cos_t, sin_t = _rope_tables(cfg)
    wc, ws, cs = _dft_tables(cfg)

    hg, ss = _prep_rows(cfg, h, pre_norm[0, 0])
    for layer in range(cfg.depth):
        i = layer // 2
        a = _ffn_up(cfg, hg, ss, wg, wu, layer, 0)
        h, hg, ss = _mm_res(cfg, [a], wd, (layer, 0), h, 0.5, next_gain=pre_norm[layer, 1])
        if layer % 2 == 0:
            u, qt, vt = _even_inproj(cfg, hg, ss, ew_in, i, cos_t, sin_t, even_q_norm[i],
                                     even_k_norm[i])
            attn = _attention(cfg, u, qt, vt)
            fa, fb = _chan_dft(cfg, u, cs)
            four = _four_assemble(cfg, *_seq_dft_half(cfg, wc, ws, fa, fb))
            h, hg, ss = _mm_res(cfg, [attn, four], ew_out, (i,), h, 1.0,
                                next_gain=pre_norm[layer, 2])
        else:
            u, low = _odd_inproj(cfg, hg, ss, ow_in, i, ga)
            rank = cfg.gla_rank
            o_fw = _gla(cfg, u, low[:, :rank], odd_gate_b[i, 0], odd_gate_bias[i, 0], rev=False)
            og = _gla(cfg, u, low[:, rank:], odd_gate_b[i, 1], odd_gate_bias[i, 1], rev=True,
                      o_fw=o_fw, head_norm=odd_head_norm[i])
            h, hg, ss = _mm_res(cfg, [og], ow_out, (i,), h, 1.0, next_gain=pre_norm[layer, 2])
        a = _ffn_up(cfg, hg, ss, wg, wu, layer, 1)
        if layer + 1 < cfg.depth:
            h, hg, ss = _mm_res(cfg, [a], wd, (layer, 1), h, 0.5,
                                next_gain=pre_norm[layer + 1, 0])
        else:
            h = _mm_res(cfg, [a], wd, (layer, 1), h, 0.5)
    return h.reshape(bsz, cfg.lp, d)[:, cfg.n_pad + cfg.n_meta:]


def kernel(x, meta_tokens, pre_norm, ffn_w_gate, ffn_w_up, ffn_w_down, even_w_in, even_q_norm,
           even_k_norm, even_w_out, odd_w_in, odd_gate_a, odd_gate_b, odd_gate_bias,
           odd_head_norm, odd_w_out):
    bsz, seq, d = x.shape
    cfg = Cfg(batch=bsz, seq=seq, d_model=d, d_ff=ffn_w_gate.shape[-1], depth=pre_norm.shape[0],
              n_meta=N_META, grid_w=GRID_W, attn_heads=ATTN_HEADS, attn_kv_heads=ATTN_KV_HEADS,
              head_dim=HEAD_DIM, q_block=Q_BLOCK, four_groups=FOURIER_GROUPS,
              four_dim=FOURIER_GROUP_DIM, gla_heads=GLA_HEADS, gla_rank=odd_gate_a.shape[-1],
              tm_res=1280, tm=1280, tn=512, attn_tk=1024, attn_unroll=8, dft_tm=1408, dft_tn=512, dft_tk=1664,
              gla_chunk=128, gla_sub=4, gla_heads_per_step=8)
    return _forward(cfg, x, meta_tokens, pre_norm, ffn_w_gate, ffn_w_up, ffn_w_down, even_w_in,
                    even_q_norm, even_k_norm, even_w_out, odd_w_in, odd_gate_a, odd_gate_b,
                    odd_gate_bias, odd_head_norm, odd_w_out)
```
